```python
import jax, jax.numpy as jnp
from jax import lax
import numpy as np

D_MODEL = 1024
BATCH = 4
SEQ = 4096
DEPTH = 4

N_MIXERS = 2
N_A = (DEPTH + 1) // 2
N_B = DEPTH // 2
LRU_WIDTH = D_MODEL
LRU_HEADS = 4
LRU_BLOCK = LRU_WIDTH // LRU_HEADS
RG_C = 8.0
CONV_WIDTH = 4
CONV_LEFT = 2
FOURIER_GROUPS = 4
FOURIER_GROUP_DIM = D_MODEL // FOURIER_GROUPS
N_EXPERTS = 16
CAPACITY_FACTOR = 2
EXPERT_FF = 2 * D_MODEL
PLE_DIM = 256
EPS = 1e-6

kernel_name = "hybrid_rglru_fnet_expert_choice_encoder"


def rms_norm(x, g):
    xf = x.astype(jnp.float32)
    y = xf * lax.rsqrt(jnp.mean(xf * xf, axis=-1, keepdims=True) + EPS)
    return (y * g.astype(jnp.float32)).astype(x.dtype)


def _lin_rec(left, right):
    a1, b1 = left
    a2, b2 = right
    return a1 * a2, a2 * b1 + b2


def rglru_mixer(xn, w_in, b_in, conv_w, conv_b, gx_w, gx_b, ga_w, ga_b, lam, w_out, b_out):
    B, S, _ = xn.shape
    proj = xn @ w_in + b_in
    gate, u = jnp.split(proj, 2, axis=-1)
    up = jnp.pad(u, ((0, 0), (CONV_LEFT, CONV_WIDTH - 1 - CONV_LEFT), (0, 0)))
    conv = up[:, 0:S] * conv_w[0]
    for k in range(1, CONV_WIDTH):
        conv = conv + up[:, k:k + S] * conv_w[k]
    uf = (conv + conv_b).astype(jnp.float32)
    ub = uf.reshape(B, S, LRU_HEADS, LRU_BLOCK)
    h_sum = jnp.zeros_like(uf)
    for d in range(2):
        i_t = jax.nn.sigmoid(jnp.einsum('bshi,hij->bshj', ub, gx_w[d].astype(jnp.float32))
                             + gx_b[d].astype(jnp.float32)).reshape(B, S, LRU_WIDTH)
        r_t = jax.nn.sigmoid(jnp.einsum('bshi,hij->bshj', ub, ga_w[d].astype(jnp.float32))
                             + ga_b[d].astype(jnp.float32)).reshape(B, S, LRU_WIDTH)
        log_a = -RG_C * r_t * jax.nn.softplus(-lam[d].astype(jnp.float32))
        a_t = jnp.exp(log_a)
        b_t = jnp.sqrt(-jnp.expm1(2.0 * log_a)) * (i_t * uf)
        _, h = lax.associative_scan(_lin_rec, (a_t, b_t), axis=1, reverse=(d == 1))
        h_sum = h_sum + h
    y = h_sum.astype(xn.dtype) * jax.nn.gelu(gate, approximate=True)
    return y @ w_out + b_out


def fourier_mixer(xn, w_in, w_out):
    B, S, D = xn.shape
    u = (xn @ w_in).reshape(B, S, FOURIER_GROUPS, FOURIER_GROUP_DIM).astype(jnp.float32)
    f = jnp.fft.fft2(u, axes=(1, 3), norm="ortho").real
    return f.reshape(B, S, D).astype(xn.dtype) @ w_out


def expert_choice_ffn(xn, w_router, w_gate, w_up, w_down):
    B, S, D = xn.shape
    cap = max(1, CAPACITY_FACTOR * S // N_EXPERTS)
    aff = jax.nn.softmax((xn @ w_router).astype(jnp.float32), axis=-1)
    gates, idx = lax.top_k(jnp.swapaxes(aff, 1, 2), cap)
    idx_flat = idx.reshape(B, N_EXPERTS * cap)
    xg = jnp.take_along_axis(xn, idx_flat[..., None], axis=1).reshape(B, N_EXPERTS, cap, D)
    hg = jax.nn.silu(jnp.einsum('becd,edf->becf', xg, w_gate)) * jnp.einsum('becd,edf->becf', xg, w_up)
    yg = jnp.einsum('becf,efd->becd', hg, w_down) * gates[..., None].astype(xn.dtype)
    out = jnp.zeros_like(xn).at[jnp.arange(B)[:, None], idx_flat].add(
        yg.reshape(B, N_EXPERTS * cap, D))
    return out


def setup_inputs(seed: int = 0) -> dict:
    key = jax.random.key(seed)
    ks = jax.random.split(key, 32)
    f32 = jnp.float32
    nrm = lambda k, shape, scale: jax.random.normal(k, shape, f32) * scale
    D, W = D_MODEL, LRU_WIDTH
    u_rad = jax.random.uniform(ks[10], (N_A, 2, W), f32, minval=0.9, maxval=0.999)
    s = u_rad ** (1.0 / RG_C)
    lam = jnp.log(s) - jnp.log1p(-s)
    return {
        "x": nrm(ks[0], (BATCH, SEQ, D), 1.0),
        "p": nrm(ks[1], (DEPTH, BATCH, SEQ, PLE_DIM), 1.0),
        "g_mix": 1.0 + nrm(ks[2], (DEPTH, D), 0.02),
        "g_ffn": 1.0 + nrm(ks[3], (DEPTH, D), 0.02),
        "g_ple": 1.0 + nrm(ks[4], (DEPTH, D), 0.02),
        "g_final": 1.0 + nrm(ks[5], (D,), 0.02),
        "rg_w_in": nrm(ks[6], (N_A, D, 2 * W), D ** -0.5),
        "rg_b_in": nrm(ks[7], (N_A, 2 * W), 0.01),
        "rg_conv_w": nrm(ks[8], (N_A, CONV_WIDTH, W), CONV_WIDTH ** -0.5),
        "rg_conv_b": nrm(ks[9], (N_A, W), 0.01),
        "rg_gx_w": nrm(ks[11], (N_A, 2, LRU_HEADS, LRU_BLOCK, LRU_BLOCK), LRU_BLOCK ** -0.5),
        "rg_gx_b": nrm(ks[12], (N_A, 2, LRU_HEADS, LRU_BLOCK), 0.01),
        "rg_ga_w": nrm(ks[13], (N_A, 2, LRU_HEADS, LRU_BLOCK, LRU_BLOCK), LRU_BLOCK ** -0.5),
        "rg_ga_b": nrm(ks[14], (N_A, 2, LRU_HEADS, LRU_BLOCK), 0.01),
        "rg_lam": lam,
        "rg_w_out": nrm(ks[15], (N_A, W, D), W ** -0.5),
        "rg_b_out": nrm(ks[16], (N_A, D), 0.01),
        "ft_w_in": nrm(ks[17], (N_B, D, D), D ** -0.5),
        "ft_w_out": nrm(ks[18], (N_B, D, D), D ** -0.5),
        "w_router": nrm(ks[19], (DEPTH, D, N_EXPERTS), D ** -0.5),
        "w_gate": nrm(ks[20], (DEPTH, N_EXPERTS, D, EXPERT_FF), D ** -0.5),
        "w_up": nrm(ks[21], (DEPTH, N_EXPERTS, D, EXPERT_FF), D ** -0.5),
        "w_down": nrm(ks[22], (DEPTH, N_EXPERTS, EXPERT_FF, D), EXPERT_FF ** -0.5),
        "ple_w_proj": nrm(ks[23], (DEPTH, PLE_DIM, D), PLE_DIM ** -0.5),
        "ple_w_gate": nrm(ks[24], (DEPTH, D, D), D ** -0.5),
    }


def reference(x, p, g_mix, g_ffn, g_ple, g_final, rg_w_in, rg_b_in, rg_conv_w, rg_conv_b,
              rg_gx_w, rg_gx_b, rg_ga_w, rg_ga_b, rg_lam, rg_w_out, rg_b_out,
              ft_w_in, ft_w_out, w_router, w_gate, w_up, w_down, ple_w_proj, ple_w_gate):
    h = x
    for i in range(DEPTH):
        xn = rms_norm(h, g_mix[i])
        j = i // N_MIXERS
        if i % N_MIXERS == 0:
            mix = rglru_mixer(xn, rg_w_in[j], rg_b_in[j], rg_conv_w[j], rg_conv_b[j],
                              rg_gx_w[j], rg_gx_b[j], rg_ga_w[j], rg_ga_b[j], rg_lam[j],
                              rg_w_out[j], rg_b_out[j])
        else:
            mix = fourier_mixer(xn, ft_w_in[j], ft_w_out[j])
        h = h + mix
        h = h + expert_choice_ffn(rms_norm(h, g_ffn[i]), w_router[i], w_gate[i], w_up[i], w_down[i])
        gate = jax.nn.sigmoid(rms_norm(h, g_ple[i]) @ ple_w_gate[i])
        h = h + gate * (p[i] @ ple_w_proj[i])
    return rms_norm(h, g_final)
```

```python
import functools
import math

import jax
import jax.numpy as jnp
from jax import lax
from jax.experimental import pallas as pl
from jax.experimental.pallas import tpu as pltpu

EPS = 1e-6
RG_C = 8.0
FOURIER_GROUPS = 4
CAPACITY_FACTOR = 2

F32 = jnp.float32
BF16 = jnp.bfloat16
I32 = jnp.int32

SUBLANES = 8
BF16_ROWS = 16
GATE_LANES = 32
GATE_SHIFT = 5
VMEM_LIMIT = 56 * 1024 * 1024


def _cparams(sem):
    return pltpu.CompilerParams(dimension_semantics=sem, vmem_limit_bytes=VMEM_LIMIT)


def _layer_spec(tail, li, nargs):
    zeros = (0,) * len(tail)
    if nargs == 1:
        return pl.BlockSpec((None,) + tail, lambda i: (li,) + zeros)
    if nargs == 2:
        return pl.BlockSpec((None,) + tail, lambda i, j: (li,) + zeros)
    return pl.BlockSpec((None,) + tail, lambda i, j, k: (li,) + zeros)


def _rms(x, g):
    return x * lax.rsqrt(jnp.mean(x * x, axis=-1, keepdims=True) + EPS) * g


def _gelu_tanh(x):
    c = math.sqrt(2.0 / math.pi)
    return 0.5 * x * (1.0 + jnp.tanh(c * (x + 0.044715 * (x * x * x))))


def _sigmoid(x):
    return 1.0 / (1.0 + jnp.exp(-x))


def _one_hot(mask):
    return jnp.where(mask, 1.0, 0.0).astype(BF16)


def _cast_rows(src_ref, dst_ref, rows=256):
    n = src_ref.shape[0]
    rows = min(rows, n)
    assert n % rows == 0

    def body(i, _):
        r = pl.multiple_of(i * rows, rows)
        dst_ref[pl.ds(r, rows), :] = src_ref[pl.ds(r, rows), :].astype(dst_ref.dtype)
        return 0

    lax.fori_loop(0, n // rows, body, 0)


def _rg_in_kernel(h_ref, g_ref, w_ref, b_ref, gg_ref, u_ref, wbf_ref):
    @pl.when(pl.program_id(0) == 0)
    def _():
        _cast_rows(w_ref, wbf_ref)

    xn = _rms(h_ref[...], g_ref[...])
    proj = jnp.dot(xn.astype(BF16), wbf_ref[...], preferred_element_type=F32) + b_ref[...]
    w = gg_ref.shape[-1]
    gg_ref[...] = _gelu_tanh(proj[:, :w])
    u_ref[...] = proj[:, w:]


def _rg_in(h, g_all, li, w_in_all, b_in_all, lj, tm):
    n, d = h.shape
    w2 = w_in_all.shape[2]
    w = w2 // 2
    return pl.pallas_call(
        _rg_in_kernel,
        grid=(n // tm,),
        in_specs=[
            pl.BlockSpec((tm, d), lambda i: (i, 0)),
            _layer_spec((1, d), li, 1),
            _layer_spec((d, w2), lj, 1),
            _layer_spec((1, w2), lj, 1),
        ],
        out_specs=[
            pl.BlockSpec((tm, w), lambda i: (i, 0)),
            pl.BlockSpec((tm, w), lambda i: (i, 0)),
        ],
        out_shape=[jax.ShapeDtypeStruct((n, w), F32), jax.ShapeDtypeStruct((n, w), F32)],
        scratch_shapes=[pltpu.VMEM((d, w2), BF16)],
        compiler_params=_cparams(("arbitrary",)),
        name="rg_in",
    )(h, g_all, w_in_all, b_in_all)


def _tile_scan(a, b, reverse):
    t = a.shape[0]
    row = lax.broadcasted_iota(I32, a.shape, 0) & (SUBLANES - 1)
    s = 1
    while s < SUBLANES:
        if reverse:
            m = row < (SUBLANES - s)
            shift = t - s
        else:
            m = row >= s
            shift = s
        a_sh = jnp.where(m, pltpu.roll(a, shift, 0), 1.0)
        b_sh = jnp.where(m, pltpu.roll(b, shift, 0), 0.0)
        b = a * b_sh + b
        a = a * a_sh
        s *= 2
    return a, b


def _rg_scan_kernel(u_ref, gg_ref, cw_ref, cb_ref, wg_ref, bg_ref, lam_ref, o_ref,
                    af_ref, bf_ref, ab_ref, bb_ref, *, chunk):
    s_len, c = u_ref.shape
    t = chunk
    n_chunks = s_len // t
    halo = SUBLANES
    lw = t + 2 * halo

    lam = -lam_ref[...]
    sp = jnp.maximum(lam, 0.0) + jnp.log1p(jnp.exp(-jnp.abs(lam)))
    cw = cw_ref[...]
    cb = cb_ref[...]
    bg = bg_ref[...]

    def pass1(ci, _):
        t0 = pl.multiple_of(ci * t, t)
        lo = pl.multiple_of(jnp.maximum(t0 - halo, 0), halo)
        hi = pl.multiple_of(jnp.minimum(t0 + t, s_len - halo), halo)
        prev = jnp.where(ci > 0, u_ref[pl.ds(lo, halo), :], 0.0)
        nxt = jnp.where(ci < n_chunks - 1, u_ref[pl.ds(hi, halo), :], 0.0)
        w = jnp.concatenate([prev, u_ref[pl.ds(t0, t), :], nxt], axis=0)
        conv = (cw[0:1] * pltpu.roll(w, 2, 0) + cw[1:2] * pltpu.roll(w, 1, 0)
                + cw[2:3] * w + cw[3:4] * pltpu.roll(w, lw - 1, 0))
        conv = conv[halo:halo + t] + cb
        gates = jnp.dot(conv.astype(BF16), wg_ref[...], preferred_element_type=F32) + bg
        for d, (a_ref, b_ref) in enumerate(((af_ref, bf_ref), (ab_ref, bb_ref))):
            i_t = _sigmoid(gates[:, (2 * d) * c:(2 * d + 1) * c])
            r_t = _sigmoid(gates[:, (2 * d + 1) * c:(2 * d + 2) * c])
            log_a = (-RG_C) * r_t * sp[d:d + 1]
            a_t = jnp.exp(log_a)
            b_t = jnp.sqrt(-jnp.tanh(log_a) * (a_t * a_t + 1.0)) * (i_t * conv)
            a_s, b_s = _tile_scan(a_t, b_t, reverse=(d == 1))
            a_ref[pl.ds(t0, t), :] = a_s
            b_ref[pl.ds(t0, t), :] = b_s
        return 0

    lax.fori_loop(0, n_chunks, pass1, 0)

    n_tiles = s_len // SUBLANES

    def pass2(k, carry):
        hf, hb = carry
        rf = pl.multiple_of(k * SUBLANES, SUBLANES)
        rb = pl.multiple_of((n_tiles - 1 - k) * SUBLANES, SUBLANES)
        h_f = bf_ref[pl.ds(rf, SUBLANES), :] + af_ref[pl.ds(rf, SUBLANES), :] * hf
        bf_ref[pl.ds(rf, SUBLANES), :] = h_f
        h_b = bb_ref[pl.ds(rb, SUBLANES), :] + ab_ref[pl.ds(rb, SUBLANES), :] * hb
        bb_ref[pl.ds(rb, SUBLANES), :] = h_b
        return h_f[SUBLANES - 1:SUBLANES, :], h_b[0:1, :]

    zero = jnp.zeros((1, c), F32)
    lax.fori_loop(0, n_tiles, pass2, (zero, zero), unroll=4)

    def pass3(ci, _):
        t0 = pl.multiple_of(ci * t, t)
        o_ref[pl.ds(t0, t), :] = ((bf_ref[pl.ds(t0, t), :] + bb_ref[pl.ds(t0, t), :])
                                  * gg_ref[pl.ds(t0, t), :])
        return 0

    lax.fori_loop(0, n_chunks, pass3, 0)


def _rg_scan(u, gg, conv_w_all, conv_b_all, wg_cat, bg_cat, lam_all, lj, batch, chunk):
    n, w = u.shape
    s_len = n // batch
    heads, c, c4 = wg_cat.shape
    kw = conv_w_all.shape[1]
    u3 = u.reshape(batch, s_len, w)
    gg3 = gg.reshape(batch, s_len, w)
    out = pl.pallas_call(
        functools.partial(_rg_scan_kernel, chunk=chunk),
        grid=(batch, heads),
        in_specs=[
            pl.BlockSpec((None, s_len, c), lambda b, h: (b, 0, h)),
            pl.BlockSpec((None, s_len, c), lambda b, h: (b, 0, h)),
            pl.BlockSpec((None, kw, c), lambda b, h: (lj, 0, h)),
            pl.BlockSpec((None, 1, c), lambda b, h: (lj, 0, h)),
            pl.BlockSpec((None, c, c4), lambda b, h: (h, 0, 0)),
            pl.BlockSpec((None, 1, c4), lambda b, h: (h, 0, 0)),
            pl.BlockSpec((None, 2, c), lambda b, h: (lj, 0, h)),
        ],
        out_specs=pl.BlockSpec((None, s_len, c), lambda b, h: (b, 0, h)),
        out_shape=jax.ShapeDtypeStruct((batch, s_len, w), F32),
        scratch_shapes=[pltpu.VMEM((s_len, c), F32)] * 4,
        compiler_params=_cparams(("arbitrary", "arbitrary")),
        name="rg_scan",
    )(u3, gg3, conv_w_all, conv_b_all, wg_cat, bg_cat, lam_all)
    return out.reshape(n, w)


def _mix_out_kernel(y_ref, h_ref, w_ref, b_ref, g_ref, wr_ref, hout_ref, xn_ref, lt_ref, wbf_ref):
    @pl.when(pl.program_id(0) == 0)
    def _():
        _cast_rows(w_ref, wbf_ref)

    hn = (h_ref[...] + jnp.dot(y_ref[...].astype(BF16), wbf_ref[...], preferred_element_type=F32)
          + b_ref[...])
    hout_ref[...] = hn
    xn = _rms(hn, g_ref[...])
    xn_ref[...] = xn
    lt_ref[...] = lax.dot_general(wr_ref[...], xn, (((1,), (1,)), ((), ())),
                                  precision=lax.Precision.HIGHEST, preferred_element_type=F32)


def _mix_out(y, h, w_all, lj, b_all, g_all, li, w_router_t_all, tm):
    n, d = h.shape
    k = y.shape[1]
    e = w_router_t_all.shape[1]
    return pl.pallas_call(
        _mix_out_kernel,
        grid=(n // tm,),
        in_specs=[
            pl.BlockSpec((tm, k), lambda i: (i, 0)),
            pl.BlockSpec((tm, d), lambda i: (i, 0)),
            _layer_spec((k, d), lj, 1),
            _layer_spec((1, d), lj, 1),
            _layer_spec((1, d), li, 1),
            _layer_spec((e, d), li, 1),
        ],
        out_specs=[
            pl.BlockSpec((tm, d), lambda i: (i, 0)),
            pl.BlockSpec((tm, d), lambda i: (i, 0)),
            pl.BlockSpec((e, tm), lambda i: (0, i)),
        ],
        out_shape=[jax.ShapeDtypeStruct((n, d), F32), jax.ShapeDtypeStruct((n, d), F32),
                   jax.ShapeDtypeStruct((e, n), F32)],
        scratch_shapes=[pltpu.VMEM((k, d), BF16)],
        compiler_params=_cparams(("arbitrary",)),
        name="mix_out",
    )(y, h, w_all, b_all, g_all, w_router_t_all)


def _excl_cumsum_lanes(x):
    s_len = x.shape[1]
    lane = lax.broadcasted_iota(I32, x.shape, 1)
    acc = x
    s = 1
    while s < s_len:
        acc = acc + jnp.where(lane >= s, pltpu.roll(acc, s, 1), 0)
        s *= 2
    return acc - x


def _route_kernel(lt_ref, ps_ref, gc_ref, *, cap):
    e, s_len = lt_ref.shape
    lg = lt_ref[...]
    ex = jnp.exp(lg - jnp.max(lg, axis=0, keepdims=True))
    aff = ex / jnp.sum(ex, axis=0, keepdims=True)

    def count(mask):
        return jnp.sum(mask.astype(F32), axis=1, keepdims=True).astype(I32)

    def radix(i, prefix):
        cand = prefix | lax.shift_left(jnp.int32(1), 30 - i)
        cand_f = lax.bitcast_convert_type(cand, F32)
        return jnp.where(count(aff >= cand_f) >= cap, cand, prefix)

    thr_bits = lax.fori_loop(0, 31, radix, jnp.zeros((e, 1), I32))
    thr = lax.bitcast_convert_type(thr_bits, F32)
    above = lax.bitcast_convert_type(thr_bits + 1, F32)
    gt = aff >= above
    eq = (aff >= thr) & jnp.logical_not(gt)
    need = cap - count(gt)
    eq_rank = _excl_cumsum_lanes(eq.astype(I32))
    sel_i = (gt | (eq & (eq_rank < need))).astype(I32)
    pos = _excl_cumsum_lanes(sel_i)
    ps_ref[...] = pos * 2 + sel_i

    rh = gc_ref.shape[1]
    row_hi = lax.broadcasted_iota(I32, (rh, s_len), 0)
    row_lo = lax.broadcasted_iota(I32, (GATE_LANES, s_len), 0)
    a1 = aff.astype(BF16).astype(F32)
    a2 = (aff - a1).astype(BF16).astype(F32)
    a3 = (aff - a1 - a2).astype(BF16).astype(F32)
    for ei in range(e):
        pe = pos[ei:ei + 1, :]
        oh_hi = ((pe >> GATE_SHIFT) == row_hi) & (sel_i[ei:ei + 1, :] > 0)
        oh_lo = _one_hot((pe & (GATE_LANES - 1)) == row_lo)
        lhs = jnp.concatenate([jnp.where(oh_hi, a[ei:ei + 1, :], 0.0) for a in (a1, a2, a3)],
                              axis=0).astype(BF16)
        r = lax.dot_general(lhs, oh_lo, (((1,), (1,)), ((), ())), preferred_element_type=F32)
        gc_ref[ei] = r[:rh] + r[rh:2 * rh] + r[2 * rh:]


def _route(lt, batch, cap):
    e, n = lt.shape
    s_len = n // batch
    assert cap % GATE_LANES == 0
    rh = max(cap // GATE_LANES, BF16_ROWS)
    return pl.pallas_call(
        functools.partial(_route_kernel, cap=cap),
        grid=(batch,),
        in_specs=[pl.BlockSpec((e, s_len), lambda b: (0, b))],
        out_specs=[
            pl.BlockSpec((None, e, s_len), lambda b: (b, 0, 0)),
            pl.BlockSpec((None, e, rh, GATE_LANES), lambda b: (b, 0, 0, 0)),
        ],
        out_shape=[
            jax.ShapeDtypeStruct((batch, e, s_len), I32),
            jax.ShapeDtypeStruct((batch, e, rh, GATE_LANES), F32),
        ],
        compiler_params=_cparams(("arbitrary",)),
        name="route",
    )(lt)


def _window_start(first_slot, align, cap, wp):
    return jnp.minimum(first_slot & (-align), cap - wp)


def _dispatch_kernel(st_ref, nr_ref, xn_ref, ps_ref, out_ref, *, w, wp):
    b = pl.program_id(0)
    i = pl.program_id(1)
    tile = b * pl.num_programs(1) + i
    e, cap, d = out_ref.shape
    t = xn_ref.shape[0]

    @pl.when(i == 0)
    def _():
        def zero(ei, _):
            out_ref[ei] = jnp.zeros((cap, d), BF16)
            return 0

        lax.fori_loop(0, e, zero, 0)

    x = xn_ref[...].astype(BF16)
    ps = ps_ref[...]
    pos = ps >> 1
    sel = (ps & 1) > 0
    start = pos[:, 0:1]
    sub = lax.broadcasted_iota(I32, (wp, t), 0)

    def one_round(q):
        lo = start + q * w
        ok = sel & (pos >= lo) & (pos < lo + w)
        prel = jnp.where(ok, pos - _window_start(lo, BF16_ROWS, cap, wp), -1)
        oh = jnp.concatenate([_one_hot(sub == prel[ei:ei + 1, :]) for ei in range(e)], axis=0)
        win = jnp.dot(oh, x, preferred_element_type=F32)
        for ei in range(e):
            ws = _window_start(st_ref[tile * e + ei] + q * w, BF16_ROWS, cap, wp)
            rows = pl.ds(pl.multiple_of(ws, BF16_ROWS), wp)
            out_ref[ei, rows, :] = (out_ref[ei, rows, :].astype(F32)
                                    + win[ei * wp:(ei + 1) * wp]).astype(BF16)

    one_round(0)

    def extra(q, _):
        one_round(q)
        return 0

    lax.fori_loop(1, nr_ref[tile], extra, 0)


def _dispatch(st_flat, nr, xn, psel, batch, cap, tt, w, wp):
    n, d = xn.shape
    s_len = n // batch
    e = psel.shape[1]
    nt = s_len // tt
    grid_spec = pltpu.PrefetchScalarGridSpec(
        num_scalar_prefetch=2,
        grid=(batch, nt),
        in_specs=[
            pl.BlockSpec((tt, d), lambda b, i, st, nr: (b * nt + i, 0)),
            pl.BlockSpec((None, e, tt), lambda b, i, st, nr: (b, 0, i)),
        ],
        out_specs=pl.BlockSpec((e, None, cap, d), lambda b, i, st, nr: (0, b, 0, 0)),
    )
    return pl.pallas_call(
        functools.partial(_dispatch_kernel, w=w, wp=wp),
        grid_spec=grid_spec,
        out_shape=jax.ShapeDtypeStruct((e, batch, cap, d), BF16),
        compiler_params=_cparams(("arbitrary", "arbitrary")),
        name="dispatch",
    )(st_flat, nr, xn, psel)


def _ffn_kernel(xg_ref, gc_ref, wg_ref, wu_ref, wd_ref, out_ref, wgb_ref, wub_ref, wdb_ref):
    fh = pl.program_id(1)
    last = pl.num_programs(1) - 1
    batch, cap, _ = xg_ref.shape

    _cast_rows(wg_ref, wgb_ref)
    _cast_rows(wu_ref, wub_ref)
    _cast_rows(wd_ref, wdb_ref)

    def per_batch(b, _):
        x = xg_ref[b]
        hgate = jnp.dot(x, wgb_ref[...], preferred_element_type=F32)
        hup = jnp.dot(x, wub_ref[...], preferred_element_type=F32)
        hg = (hgate * _sigmoid(hgate) * hup).astype(BF16)
        y = jnp.dot(hg, wdb_ref[...], preferred_element_type=F32)
        rows = pl.ds(pl.multiple_of(b * cap, cap), cap)

        @pl.when(fh > 0)
        def _():
            out_ref[rows, :] += y

        @pl.when(fh == 0)
        def _():
            out_ref[rows, :] = y

        @pl.when(fh == last)
        def _():
            out_ref[rows, :] = out_ref[rows, :] * gc_ref[rows, :]

        return 0

    lax.fori_loop(0, batch, per_batch, 0)


def _ffn(xg, gc, w_gate_all, w_up_all, w_down_all, li, n_fh):
    e, batch, cap, d = xg.shape
    f = w_gate_all.shape[3]
    fh = f // n_fh
    return pl.pallas_call(
        _ffn_kernel,
        grid=(e, n_fh),
        in_specs=[
            pl.BlockSpec((None, batch, cap, d), lambda ei, fi: (ei, 0, 0, 0)),
            pl.BlockSpec((None, batch * cap, 1), lambda ei, fi: (ei, 0, 0)),
            pl.BlockSpec((None, None, d, fh), lambda ei, fi: (li, ei, 0, fi)),
            pl.BlockSpec((None, None, d, fh), lambda ei, fi: (li, ei, 0, fi)),
            pl.BlockSpec((None, None, fh, d), lambda ei, fi: (li, ei, fi, 0)),
        ],
        out_specs=pl.BlockSpec((None, batch * cap, d), lambda ei, fi: (ei, 0, 0)),
        out_shape=jax.ShapeDtypeStruct((e, batch * cap, d), F32),
        scratch_shapes=[
            pltpu.VMEM((d, fh), BF16),
            pltpu.VMEM((d, fh), BF16),
            pltpu.VMEM((fh, d), BF16),
        ],
        compiler_params=_cparams(("arbitrary", "arbitrary")),
        name="expert_ffn",
    )(xg, gc, w_gate_all, w_up_all, w_down_all)


def _combine_kernel(st_ref, nr_ref, h_ref, ps_ref, p_ref, yg_hbm, wp_ref, wgt_ref, gple_ref,
                    gfin_ref, hout_ref, slab_ref, acc_ref, wpb_ref, wgb_ref, sem,
                    *, final, w, wp, cap, batch, tiles_per_batch):
    i = pl.program_id(0)
    t, e = ps_ref.shape
    slot = i & 1

    def window_copies(tile, q, buf):
        b = tile // tiles_per_batch
        copies = []
        for ei in range(e):
            ws = _window_start(st_ref[tile * e + ei] + q * w, SUBLANES, cap, wp)
            row0 = pl.multiple_of((ei * batch + b) * cap + ws, SUBLANES)
            copies.append(pltpu.make_async_copy(
                yg_hbm.at[pl.ds(row0, wp), :], slab_ref.at[buf, pl.ds(ei * wp, wp), :],
                sem.at[buf]))
        return copies

    @pl.when(i == 0)
    def _():
        _cast_rows(wp_ref, wpb_ref)
        _cast_rows(wgt_ref, wgb_ref)
        for c in window_copies(0, 0, 0):
            c.start()

    @pl.when(i + 1 < pl.num_programs(0))
    def _():
        for c in window_copies(i + 1, 0, 1 - slot):
            c.start()

    ps = ps_ref[...]
    pos = ps >> 1
    sel = (ps & 1) > 0
    start = pos[0:1, :]
    k = e * wp
    wp_shift = wp.bit_length() - 1
    expand = _one_hot(lax.broadcasted_iota(I32, (e, k), 1) >> wp_shift
                      == lax.broadcasted_iota(I32, (e, k), 0))
    lane_w = (lax.broadcasted_iota(I32, (t, k), 1) & (wp - 1)).astype(F32)

    def apply_round(q):
        lo = start + q * w
        ok = sel & (pos >= lo) & (pos < lo + w)
        prel = jnp.where(ok, pos - _window_start(lo, SUBLANES, cap, wp), -1)
        pexp = jnp.dot(prel.astype(F32).astype(BF16), expand, preferred_element_type=F32)
        oh = _one_hot(pexp == lane_w)
        s = slab_ref[slot]
        s_hi = s.astype(BF16)
        s_lo = (s - s_hi.astype(F32)).astype(BF16)
        acc_ref[...] += (jnp.dot(oh, s_hi, preferred_element_type=F32)
                         + jnp.dot(oh, s_lo, preferred_element_type=F32))

    acc_ref[...] = h_ref[...]
    for c in window_copies(i, 0, slot):
        c.wait()
    apply_round(0)

    def extra(q, _):
        for c in window_copies(i, q, slot):
            c.start()
        for c in window_copies(i, q, slot):
            c.wait()
        apply_round(q)
        return 0

    lax.fori_loop(1, nr_ref[i], extra, 0)

    hn = acc_ref[...]
    gate = _sigmoid(jnp.dot(_rms(hn, gple_ref[...]).astype(BF16), wgb_ref[...],
                            preferred_element_type=F32))
    hn = hn + gate * jnp.dot(p_ref[...].astype(BF16), wpb_ref[...], preferred_element_type=F32)
    if final:
        hn = _rms(hn, gfin_ref[...])
    hout_ref[...] = hn


def _combine(st_flat, nr, h, psel_tm, p_all, yg, w_proj_all, w_gate_all, g_ple_all, li, g_final,
             batch, cap, tt, w, wp, final):
    n, d = h.shape
    e = psel_tm.shape[1]
    pd = p_all.shape[2]
    nt = n // tt
    grid_spec = pltpu.PrefetchScalarGridSpec(
        num_scalar_prefetch=2,
        grid=(nt,),
        in_specs=[
            pl.BlockSpec((tt, d), lambda i, st, nr: (i, 0)),
            pl.BlockSpec((tt, e), lambda i, st, nr: (i, 0)),
            pl.BlockSpec((None, tt, pd), lambda i, st, nr: (li, i, 0)),
            pl.BlockSpec(memory_space=pl.ANY),
            pl.BlockSpec((None, pd, d), lambda i, st, nr: (li, 0, 0)),
            pl.BlockSpec((None, d, d), lambda i, st, nr: (li, 0, 0)),
            pl.BlockSpec((None, 1, d), lambda i, st, nr: (li, 0, 0)),
            pl.BlockSpec((1, d), lambda i, st, nr: (0, 0)),
        ],
        out_specs=pl.BlockSpec((tt, d), lambda i, st, nr: (i, 0)),
        scratch_shapes=[
            pltpu.VMEM((2, e * wp, d), F32),
            pltpu.VMEM((tt, d), F32),
            pltpu.VMEM((pd, d), BF16),
            pltpu.VMEM((d, d), BF16),
            pltpu.SemaphoreType.DMA((2,)),
        ],
    )
    return pl.pallas_call(
        functools.partial(_combine_kernel, final=final, w=w, wp=wp, cap=cap, batch=batch,
                          tiles_per_batch=nt // batch),
        grid_spec=grid_spec,
        out_shape=jax.ShapeDtypeStruct((n, d), F32),
        compiler_params=_cparams(("arbitrary",)),
        name="combine_ple",
    )(st_flat, nr, h, psel_tm, p_all, yg, w_proj_all, w_gate_all, g_ple_all, g_final.reshape(1, d))


def _fourier_in_kernel(h_ref, g_ref, w_ref, cs_ref, out_ref, wbf_ref, *, groups):
    @pl.when((pl.program_id(0) == 0) & (pl.program_id(1) == 0))
    def _():
        _cast_rows(w_ref, wbf_ref)

    xn = _rms(h_ref[...], g_ref[...])
    u = jnp.dot(xn.astype(BF16), wbf_ref[...], preferred_element_type=F32).astype(BF16)
    gd = u.shape[1] // groups
    for gi in range(groups):
        r = jnp.dot(u[:, gi * gd:(gi + 1) * gd], cs_ref[...], preferred_element_type=F32)
        out_ref[0, :, gi * gd:(gi + 1) * gd] = r[:, :gd].astype(BF16)
        out_ref[1, :, gi * gd:(gi + 1) * gd] = r[:, gd:].astype(BF16)


def _fourier_in(h, g_all, li, w_in_all, lj, cs_tab, batch, tm):
    n, d = h.shape
    s_len = n // batch
    nt = s_len // tm
    return pl.pallas_call(
        functools.partial(_fourier_in_kernel, groups=FOURIER_GROUPS),
        grid=(batch, nt),
        in_specs=[
            pl.BlockSpec((tm, d), lambda b, i: (b * nt + i, 0)),
            _layer_spec((1, d), li, 2),
            _layer_spec((d, d), lj, 2),
            pl.BlockSpec(cs_tab.shape, lambda b, i: (0, 0)),
        ],
        out_specs=pl.BlockSpec((None, 2, tm, d), lambda b, i: (b, 0, i, 0)),
        out_shape=jax.ShapeDtypeStruct((batch, 2, s_len, d), BF16),
        scratch_shapes=[pltpu.VMEM((d, d), BF16)],
        compiler_params=_cparams(("arbitrary", "arbitrary")),
        name="fourier_in",
    )(h, g_all, w_in_all, cs_tab)


def _seq_dft_kernel(tab_ref, rhs_ref, out_ref, acc_ref, *, scale):
    k = pl.program_id(2)

    @pl.when(k == 0)
    def _():
        acc_ref[...] = jnp.zeros(acc_ref.shape, F32)

    acc_ref[...] += jnp.dot(tab_ref[...], rhs_ref[...], preferred_element_type=F32)

    @pl.when(k == pl.num_programs(2) - 1)
    def _():
        out_ref[...] = (acc_ref[...] * scale).astype(out_ref.dtype)


def _seq_dft(ucs, tab, scale, tm, tk):
    batch, two_s, d = ucs.shape
    s_len = two_s // 2
    return pl.pallas_call(
        functools.partial(_seq_dft_kernel, scale=scale),
        grid=(batch, s_len // tm, two_s // tk),
        in_specs=[
            pl.BlockSpec((tm, tk), lambda b, i, k: (i, k)),
            pl.BlockSpec((None, tk, d), lambda b, i, k: (b, k, 0)),
        ],
        out_specs=pl.BlockSpec((None, tm, d), lambda b, i, k: (b, i, 0)),
        out_shape=jax.ShapeDtypeStruct((batch, s_len, d), BF16),
        scratch_shapes=[pltpu.VMEM((tm, d), F32)],
        compiler_params=_cparams(("arbitrary", "arbitrary", "arbitrary")),
        name="seq_dft",
    )(tab, ucs)


def _dft_tables(s_len, gd):
    def cs(n):
        i = lax.broadcasted_iota(I32, (n, n), 0)
        j = lax.broadcasted_iota(I32, (n, n), 1)
        ang = ((i * j) % n).astype(F32) * (2.0 * math.pi / n)
        return jnp.cos(ang), jnp.sin(ang)

    c_s, s_s = cs(s_len)
    c_c, s_c = cs(gd)
    return (jnp.concatenate([c_s, -s_s], axis=1).astype(BF16),
            jnp.concatenate([c_c, s_c], axis=1).astype(BF16))


def _tile_tables(psel, cap, tt, w_dispatch, w_combine):
    batch, e, s_len = psel.shape
    st = (psel >> 1)[:, :, ::tt]
    nxt = jnp.concatenate([st[:, :, 1:], jnp.full((batch, e, 1), cap, I32)], axis=2)
    most = jnp.max(nxt - st, axis=1).reshape(-1)
    rounds = lambda w: jnp.maximum((most + (w - 1)) // w, 1)
    return jnp.swapaxes(st, 1, 2).reshape(-1), rounds(w_dispatch), rounds(w_combine)


def kernel(x, p, g_mix, g_ffn, g_ple, g_final, rg_w_in, rg_b_in, rg_conv_w, rg_conv_b,
           rg_gx_w, rg_gx_b, rg_ga_w, rg_ga_b, rg_lam, rg_w_out, rg_b_out,
           ft_w_in, ft_w_out, w_router, w_gate, w_up, w_down, ple_w_proj, ple_w_gate):
    batch, s_len, d = x.shape
    n = batch * s_len
    depth = g_mix.shape[0]
    n_exp = w_router.shape[2]
    cap = max(1, CAPACITY_FACTOR * s_len // n_exp)
    gd = d // FOURIER_GROUPS
    tm = min(512, s_len)
    tt = min(256, s_len)
    chunk = min(512, s_len)
    wp = max(2 * tt * cap // s_len, 2 * BF16_ROWS)
    assert wp & (wp - 1) == 0 and wp <= cap
    w_dispatch, w_combine = wp - BF16_ROWS, wp - SUBLANES

    h = x.reshape(n, d)
    seq_tab, ch_tab = _dft_tables(s_len, gd)
    dft_scale = 1.0 / math.sqrt(s_len * gd)
    row = lambda a: a.reshape(a.shape[0], 1, a.shape[-1])
    g_mix3, g_ffn3, g_ple3 = row(g_mix), row(g_ffn), row(g_ple)
    w_router_t = jnp.swapaxes(w_router, 1, 2)
    p3 = p.reshape(depth, n, p.shape[-1])
    rg_b_in3, rg_conv_b3, rg_b_out3 = row(rg_b_in), row(rg_conv_b), row(rg_b_out)
    ft_b_out3 = jnp.zeros((ft_w_out.shape[0], 1, d), F32)

    for i in range(depth):
        j = i // 2
        if i % 2 == 0:
            gg, u = _rg_in(h, g_mix3, i, rg_w_in, rg_b_in3, j, tm)
            gx, ga = rg_gx_w[j], rg_ga_w[j]
            wg_cat = jnp.concatenate([gx[0], ga[0], gx[1], ga[1]], axis=-1).astype(BF16)
            gxb, gab = rg_gx_b[j], rg_ga_b[j]
            bg_cat = jnp.concatenate([gxb[0], gab[0], gxb[1], gab[1]], axis=-1)[:, None, :]
            y = _rg_scan(u, gg, rg_conv_w, rg_conv_b3, wg_cat, bg_cat, rg_lam, j, batch, chunk)
            h, xn, lt = _mix_out(y, h, rg_w_out, j, rg_b_out3, g_ffn3, i, w_router_t, tm)
        else:
            ucs = _fourier_in(h, g_mix3, i, ft_w_in, j, ch_tab, batch, tm)
            f = _seq_dft(ucs.reshape(batch, 2 * s_len, d), seq_tab, dft_scale,
                         min(1024, s_len), min(1024, s_len))
            h, xn, lt = _mix_out(f.reshape(n, d), h, ft_w_out, j, ft_b_out3, g_ffn3, i,
                                 w_router_t, tm)

        psel, gc = _route(lt, batch, cap)
        st_flat, nr_d, nr_c = _tile_tables(psel, cap, tt, w_dispatch, w_combine)
        gc = jnp.swapaxes(gc[:, :, :cap // GATE_LANES, :].reshape(batch, n_exp, cap),
                          0, 1).reshape(n_exp, batch * cap, 1)
        psel_tm = jnp.swapaxes(psel, 1, 2).reshape(n, n_exp)
        xg = _dispatch(st_flat, nr_d, xn, psel, batch, cap, tt, w_dispatch, wp)
        yg = _ffn(xg, gc, w_gate, w_up, w_down, i, 4)
        h = _combine(st_flat, nr_c, h, psel_tm, p3, yg.reshape(n_exp * batch * cap, d),
                     ple_w_proj, ple_w_gate, g_ple3, i, g_final, batch, cap, tt,
                     w_combine, wp, final=(i == depth - 1))
    return h.reshape(batch, s_len, d)
```

```python
import functools
import math

import jax
import jax.numpy as jnp
import numpy as np
from jax import lax
from jax.experimental import pallas as pl
from jax.experimental.pallas import tpu as pltpu

EPS = 1e-6
RG_C = 8.0
FOURIER_GROUPS = 4
CAPACITY_FACTOR = 2

F32 = jnp.float32
BF16 = jnp.bfloat16
I32 = jnp.int32

SUBLANES = 8
BF16_ROWS = 16
GATE_LANES = 32
GATE_SHIFT = 5
VMEM_LIMIT = 56 * 1024 * 1024


def _cparams(sem):
    return pltpu.CompilerParams(dimension_semantics=sem, vmem_limit_bytes=VMEM_LIMIT)


def _layer_spec(tail, li, nargs):
    zeros = (0,) * len(tail)
    if nargs == 1:
        return pl.BlockSpec((None,) + tail, lambda i: (li,) + zeros)
    if nargs == 2:
        return pl.BlockSpec((None,) + tail, lambda i, j: (li,) + zeros)
    return pl.BlockSpec((None,) + tail, lambda i, j, k: (li,) + zeros)


def _rms(x, g):
    return x * lax.rsqrt(jnp.mean(x * x, axis=-1, keepdims=True) + EPS) * g


def _gelu_tanh(x):
    c = math.sqrt(2.0 / math.pi)
    return 0.5 * x * (1.0 + jnp.tanh(c * (x + 0.044715 * (x * x * x))))


def _sigmoid(x):
    return 1.0 / (1.0 + jnp.exp(-x))


def _one_hot(mask):
    return jnp.where(mask, 1.0, 0.0).astype(BF16)


def _cast_rows(src_ref, dst_ref, rows=256):
    n = src_ref.shape[0]
    rows = min(rows, n)
    assert n % rows == 0

    def body(i, _):
        r = pl.multiple_of(i * rows, rows)
        dst_ref[pl.ds(r, rows), :] = src_ref[pl.ds(r, rows), :].astype(dst_ref.dtype)
        return 0

    lax.fori_loop(0, n // rows, body, 0)


def _rg_in_kernel(h_ref, g_ref, w_ref, b_ref, gg_ref, u_ref, wbf_ref):
    @pl.when(pl.program_id(0) == 0)
    def _():
        _cast_rows(w_ref, wbf_ref)

    xn = _rms(h_ref[...], g_ref[...])
    proj = jnp.dot(xn.astype(BF16), wbf_ref[...], preferred_element_type=F32) + b_ref[...]
    w = gg_ref.shape[-1]
    gg_ref[...] = _gelu_tanh(proj[:, :w])
    u_ref[...] = proj[:, w:]


def _rg_in(h, g_all, li, w_in_all, b_in_all, lj, tm):
    n, d = h.shape
    w2 = w_in_all.shape[2]
    w = w2 // 2
    return pl.pallas_call(
        _rg_in_kernel,
        grid=(n // tm,),
        in_specs=[
            pl.BlockSpec((tm, d), lambda i: (i, 0)),
            _layer_spec((1, d), li, 1),
            _layer_spec((d, w2), lj, 1),
            _layer_spec((1, w2), lj, 1),
        ],
        out_specs=[
            pl.BlockSpec((tm, w), lambda i: (i, 0)),
            pl.BlockSpec((tm, w), lambda i: (i, 0)),
        ],
        out_shape=[jax.ShapeDtypeStruct((n, w), F32), jax.ShapeDtypeStruct((n, w), F32)],
        scratch_shapes=[pltpu.VMEM((d, w2), BF16)],
        compiler_params=_cparams(("arbitrary",)),
        name="rg_in",
    )(h, g_all, w_in_all, b_in_all)


def _tile_scan(a, b, reverse):
    t, c = a.shape
    a = a.reshape(t // SUBLANES, SUBLANES, c)
    b = b.reshape(t // SUBLANES, SUBLANES, c)
    row = lax.broadcasted_iota(I32, a.shape, 1)
    s = 1
    while s < SUBLANES:
        if reverse:
            m = row < (SUBLANES - s)
            shift = SUBLANES - s
        else:
            m = row >= s
            shift = s
        a_sh = jnp.where(m, pltpu.roll(a, shift, 1), 1.0)
        b_sh = jnp.where(m, pltpu.roll(b, shift, 1), 0.0)
        b = a * b_sh + b
        a = a * a_sh
        s *= 2
    return a.reshape(t, c), b.reshape(t, c)


def _rg_scan_kernel(u_ref, gg_ref, cw_ref, cb_ref, wg_ref, bg_ref, lam_ref, o_ref,
                    af_ref, bf_ref, ab_ref, bb_ref, *, chunk):
    s_len, c = u_ref.shape
    t = chunk
    n_chunks = s_len // t
    halo = SUBLANES
    lw = t + 2 * halo

    lam = -lam_ref[...]
    sp = jnp.maximum(lam, 0.0) + jnp.log1p(jnp.exp(-jnp.abs(lam)))
    cw = cw_ref[...]
    cb = cb_ref[...]
    bg = bg_ref[...]

    def pass1(ci, _):
        t0 = pl.multiple_of(ci * t, t)
        lo = pl.multiple_of(jnp.maximum(t0 - halo, 0), halo)
        hi = pl.multiple_of(jnp.minimum(t0 + t, s_len - halo), halo)
        prev = jnp.where(ci > 0, u_ref[pl.ds(lo, halo), :], 0.0)
        nxt = jnp.where(ci < n_chunks - 1, u_ref[pl.ds(hi, halo), :], 0.0)
        w = jnp.concatenate([prev, u_ref[pl.ds(t0, t), :], nxt], axis=0)
        conv = (cw[0:1] * pltpu.roll(w, 2, 0) + cw[1:2] * pltpu.roll(w, 1, 0)
                + cw[2:3] * w + cw[3:4] * pltpu.roll(w, lw - 1, 0))
        conv = conv[halo:halo + t] + cb
        gates = jnp.dot(conv.astype(BF16), wg_ref[...], preferred_element_type=F32) + bg
        for d, (a_ref, b_ref) in enumerate(((af_ref, bf_ref), (ab_ref, bb_ref))):
            i_t = _sigmoid(gates[:, (2 * d) * c:(2 * d + 1) * c])
            r_t = _sigmoid(gates[:, (2 * d + 1) * c:(2 * d + 2) * c])
            log_a = (-RG_C) * r_t * sp[d:d + 1]
            a_t = jnp.exp(log_a)
            b_t = jnp.sqrt(-jnp.tanh(log_a) * (a_t * a_t + 1.0)) * (i_t * conv)
            a_s, b_s = _tile_scan(a_t, b_t, reverse=(d == 1))
            a_ref[pl.ds(t0, t), :] = a_s
            b_ref[pl.ds(t0, t), :] = b_s
        return 0

    lax.fori_loop(0, n_chunks, pass1, 0)

    n_tiles = s_len // SUBLANES

    def pass2(k, carry):
        hf, hb = carry
        rf = pl.multiple_of(k * SUBLANES, SUBLANES)
        rb = pl.multiple_of((n_tiles - 1 - k) * SUBLANES, SUBLANES)
        h_f = bf_ref[pl.ds(rf, SUBLANES), :] + af_ref[pl.ds(rf, SUBLANES), :] * hf
        bf_ref[pl.ds(rf, SUBLANES), :] = h_f
        h_b = bb_ref[pl.ds(rb, SUBLANES), :] + ab_ref[pl.ds(rb, SUBLANES), :] * hb
        bb_ref[pl.ds(rb, SUBLANES), :] = h_b
        return h_f[SUBLANES - 1:SUBLANES, :], h_b[0:1, :]

    zero = jnp.zeros((1, c), F32)
    lax.fori_loop(0, n_tiles, pass2, (zero, zero), unroll=4)

    def pass3(ci, _):
        t0 = pl.multiple_of(ci * t, t)
        o_ref[pl.ds(t0, t), :] = ((bf_ref[pl.ds(t0, t), :] + bb_ref[pl.ds(t0, t), :])
                                  * gg_ref[pl.ds(t0, t), :]).astype(o_ref.dtype)
        return 0

    lax.fori_loop(0, n_chunks, pass3, 0)


def _rg_scan(u, gg, conv_w_all, conv_b_all, wg_cat, bg_cat, lam_all, lj, batch, chunk):
    n, w = u.shape
    s_len = n // batch
    heads, c, c4 = wg_cat.shape
    kw = conv_w_all.shape[1]
    u3 = u.reshape(batch, s_len, w)
    gg3 = gg.reshape(batch, s_len, w)
    out = pl.pallas_call(
        functools.partial(_rg_scan_kernel, chunk=chunk),
        grid=(batch, heads),
        in_specs=[
            pl.BlockSpec((None, s_len, c), lambda b, h: (b, 0, h)),
            pl.BlockSpec((None, s_len, c), lambda b, h: (b, 0, h)),
            pl.BlockSpec((None, kw, c), lambda b, h: (lj, 0, h)),
            pl.BlockSpec((None, 1, c), lambda b, h: (lj, 0, h)),
            pl.BlockSpec((None, c, c4), lambda b, h: (h, 0, 0)),
            pl.BlockSpec((None, 1, c4), lambda b, h: (h, 0, 0)),
            pl.BlockSpec((None, 2, c), lambda b, h: (lj, 0, h)),
        ],
        out_specs=pl.BlockSpec((None, s_len, c), lambda b, h: (b, 0, h)),
        out_shape=jax.ShapeDtypeStruct((batch, s_len, w), BF16),
        scratch_shapes=[pltpu.VMEM((s_len, c), F32)] * 4,
        compiler_params=_cparams(("arbitrary", "arbitrary")),
        name="rg_scan",
    )(u3, gg3, conv_w_all, conv_b_all, wg_cat, bg_cat, lam_all)
    return out.reshape(n, w)


def _mix_out_kernel(y_ref, h_ref, w_ref, b_ref, g_ref, wr_ref, hout_ref, xn_ref, lt_ref, wbf_ref):
    @pl.when(pl.program_id(0) == 0)
    def _():
        _cast_rows(w_ref, wbf_ref)

    hn = (h_ref[...] + jnp.dot(y_ref[...].astype(BF16), wbf_ref[...], preferred_element_type=F32)
          + b_ref[...])
    hout_ref[...] = hn
    xn = _rms(hn, g_ref[...])
    xn_ref[...] = xn.astype(BF16)
    lt_ref[...] = lax.dot_general(wr_ref[...], xn, (((1,), (1,)), ((), ())),
                                  precision=lax.Precision.HIGHEST, preferred_element_type=F32)


def _mix_out(y, h, w_all, lj, b_all, g_all, li, w_router_t_all, tm):
    n, d = h.shape
    k = y.shape[1]
    e = w_router_t_all.shape[1]
    return pl.pallas_call(
        _mix_out_kernel,
        grid=(n // tm,),
        in_specs=[
            pl.BlockSpec((tm, k), lambda i: (i, 0)),
            pl.BlockSpec((tm, d), lambda i: (i, 0)),
            _layer_spec((k, d), lj, 1),
            _layer_spec((1, d), lj, 1),
            _layer_spec((1, d), li, 1),
            _layer_spec((e, d), li, 1),
        ],
        out_specs=[
            pl.BlockSpec((tm, d), lambda i: (i, 0)),
            pl.BlockSpec((tm, d), lambda i: (i, 0)),
            pl.BlockSpec((e, tm), lambda i: (0, i)),
        ],
        out_shape=[jax.ShapeDtypeStruct((n, d), F32), jax.ShapeDtypeStruct((n, d), BF16),
                   jax.ShapeDtypeStruct((e, n), F32)],
        scratch_shapes=[pltpu.VMEM((k, d), BF16)],
        compiler_params=_cparams(("arbitrary",)),
        name="mix_out",
    )(y, h, w_all, b_all, g_all, w_router_t_all)


def _excl_cumsum_lanes(x):
    s_len = x.shape[1]
    lane = lax.broadcasted_iota(I32, x.shape, 1)
    acc = x
    s = 1
    while s < s_len:
        acc = acc + jnp.where(lane >= s, pltpu.roll(acc, s, 1), 0)
        s *= 2
    return acc - x


def _route_kernel(lt_ref, ps_ref, gc_ref, *, cap):
    e, s_len = lt_ref.shape
    lg = lt_ref[...]
    ex = jnp.exp(lg - jnp.max(lg, axis=0, keepdims=True))
    aff = ex / jnp.sum(ex, axis=0, keepdims=True)

    def count(mask):
        return jnp.sum(mask.astype(F32), axis=1, keepdims=True).astype(I32)

    def radix(i, prefix):
        cand = prefix | lax.shift_left(jnp.int32(1), 30 - i)
        cand_f = lax.bitcast_convert_type(cand, F32)
        return jnp.where(count(aff >= cand_f) >= cap, cand, prefix)

    thr_bits = lax.fori_loop(0, 31, radix, jnp.zeros((e, 1), I32))
    thr = lax.bitcast_convert_type(thr_bits, F32)
    above = lax.bitcast_convert_type(thr_bits + 1, F32)
    gt = aff >= above
    eq = (aff >= thr) & jnp.logical_not(gt)
    need = cap - count(gt)
    eq_rank = _excl_cumsum_lanes(eq.astype(I32))
    sel_i = (gt | (eq & (eq_rank < need))).astype(I32)
    pos = _excl_cumsum_lanes(sel_i)
    ps_ref[...] = pos * 2 + sel_i

    rh = gc_ref.shape[1]
    row_hi = lax.broadcasted_iota(I32, (rh, s_len), 0)
    row_lo = lax.broadcasted_iota(I32, (GATE_LANES, s_len), 0)
    a1 = aff.astype(BF16).astype(F32)
    a2 = (aff - a1).astype(BF16).astype(F32)
    a3 = (aff - a1 - a2).astype(BF16).astype(F32)
    for ei in range(e):
        pe = pos[ei:ei + 1, :]
        oh_hi = ((pe >> GATE_SHIFT) == row_hi) & (sel_i[ei:ei + 1, :] > 0)
        oh_lo = _one_hot((pe & (GATE_LANES - 1)) == row_lo)
        lhs = jnp.concatenate([jnp.where(oh_hi, a[ei:ei + 1, :], 0.0) for a in (a1, a2, a3)],
                              axis=0).astype(BF16)
        r = lax.dot_general(lhs, oh_lo, (((1,), (1,)), ((), ())), preferred_element_type=F32)
        gc_ref[ei] = r[:rh] + r[rh:2 * rh] + r[2 * rh:]


def _route(lt, batch, cap):
    e, n = lt.shape
    s_len = n // batch
    assert cap % GATE_LANES == 0
    rh = max(cap // GATE_LANES, BF16_ROWS)
    return pl.pallas_call(
        functools.partial(_route_kernel, cap=cap),
        grid=(batch,),
        in_specs=[pl.BlockSpec((e, s_len), lambda b: (0, b))],
        out_specs=[
            pl.BlockSpec((None, e, s_len), lambda b: (b, 0, 0)),
            pl.BlockSpec((None, e, rh, GATE_LANES), lambda b: (b, 0, 0, 0)),
        ],
        out_shape=[
            jax.ShapeDtypeStruct((batch, e, s_len), I32),
            jax.ShapeDtypeStruct((batch, e, rh, GATE_LANES), F32),
        ],
        compiler_params=_cparams(("arbitrary",)),
        name="route",
    )(lt)


def _window_start(first_slot, align, cap, wp):
    return jnp.minimum(first_slot & (-align), cap - wp)


def _dispatch_kernel(st_ref, nr_ref, xn_ref, ps_ref, out_ref, *, w, wp):
    b = pl.program_id(0)
    i = pl.program_id(1)
    tile = b * pl.num_programs(1) + i
    e, cap, d = out_ref.shape
    t = xn_ref.shape[0]

    @pl.when(i == 0)
    def _():
        def zero(ei, _):
            out_ref[ei] = jnp.zeros((cap, d), BF16)
            return 0

        lax.fori_loop(0, e, zero, 0)

    x = xn_ref[...]
    ps = ps_ref[...]
    pos = ps >> 1
    sel = (ps & 1) > 0
    start = pos[:, 0:1]
    sub = lax.broadcasted_iota(I32, (wp, t), 0)

    def one_round(q):
        lo = start + q * w
        ok = sel & (pos >= lo) & (pos < lo + w)
        prel = jnp.where(ok, pos - _window_start(lo, BF16_ROWS, cap, wp), -1)
        oh = jnp.concatenate([_one_hot(sub == prel[ei:ei + 1, :]) for ei in range(e)], axis=0)
        win = jnp.dot(oh, x, preferred_element_type=F32)
        for ei in range(e):
            ws = _window_start(st_ref[tile * e + ei] + q * w, BF16_ROWS, cap, wp)
            rows = pl.ds(pl.multiple_of(ws, BF16_ROWS), wp)
            out_ref[ei, rows, :] = out_ref[ei, rows, :] + win[ei * wp:(ei + 1) * wp].astype(BF16)

    one_round(0)

    def extra(q, _):
        one_round(q)
        return 0

    lax.fori_loop(1, nr_ref[tile], extra, 0)


def _dispatch(st_flat, nr, xn, psel, batch, cap, tt, w, wp):
    n, d = xn.shape
    s_len = n // batch
    e = psel.shape[1]
    nt = s_len // tt
    grid_spec = pltpu.PrefetchScalarGridSpec(
        num_scalar_prefetch=2,
        grid=(batch, nt),
        in_specs=[
            pl.BlockSpec((tt, d), lambda b, i, st, nr: (b * nt + i, 0)),
            pl.BlockSpec((None, e, tt), lambda b, i, st, nr: (b, 0, i)),
        ],
        out_specs=pl.BlockSpec((e, None, cap, d), lambda b, i, st, nr: (0, b, 0, 0)),
    )
    return pl.pallas_call(
        functools.partial(_dispatch_kernel, w=w, wp=wp),
        grid_spec=grid_spec,
        out_shape=jax.ShapeDtypeStruct((e, batch, cap, d), BF16),
        compiler_params=_cparams(("arbitrary", "arbitrary")),
        name="dispatch",
    )(st_flat, nr, xn, psel)


def _ffn_kernel(xg_ref, gc_ref, wg_ref, wu_ref, wd_ref, hi_ref, lo_ref,
                wgb_ref, wub_ref, wdb_ref, acc_ref):
    fh = pl.program_id(1)
    batch, cap, _ = xg_ref.shape

    _cast_rows(wg_ref, wgb_ref)
    _cast_rows(wu_ref, wub_ref)
    _cast_rows(wd_ref, wdb_ref)

    for b in range(batch):
        x = xg_ref[b]
        hgate = jnp.dot(x, wgb_ref[...], preferred_element_type=F32)
        hup = jnp.dot(x, wub_ref[...], preferred_element_type=F32)
        hg = (hgate * _sigmoid(hgate) * hup).astype(BF16)
        y = jnp.dot(hg, wdb_ref[...], preferred_element_type=F32)
        rows = pl.ds(b * cap, cap)

        @pl.when(fh == 0)
        def _():
            acc_ref[rows, :] = y

        @pl.when(fh > 0)
        def _():
            acc_ref[rows, :] += y

    @pl.when(fh == pl.num_programs(1) - 1)
    def _():
        def split(i, _):
            rows = pl.ds(pl.multiple_of(i * cap, cap), cap)
            v = acc_ref[rows, :] * gc_ref[rows, :]
            hi = v.astype(BF16)
            hi_ref[rows, :] = hi
            lo_ref[rows, :] = (v - hi.astype(F32)).astype(BF16)
            return 0

        lax.fori_loop(0, batch, split, 0)


def _ffn(xg, gc, w_gate_all, w_up_all, w_down_all, li, n_fh):
    e, batch, cap, d = xg.shape
    f = w_gate_all.shape[3]
    fh = f // n_fh
    out = jax.ShapeDtypeStruct((e, batch * cap, d), BF16)
    out_spec = pl.BlockSpec((None, batch * cap, d), lambda ei, fi: (ei, 0, 0))
    return pl.pallas_call(
        _ffn_kernel,
        grid=(e, n_fh),
        in_specs=[
            pl.BlockSpec((None, batch, cap, d), lambda ei, fi: (ei, 0, 0, 0)),
            pl.BlockSpec((None, batch * cap, 1), lambda ei, fi: (ei, 0, 0)),
            pl.BlockSpec((None, None, d, fh), lambda ei, fi: (li, ei, 0, fi)),
            pl.BlockSpec((None, None, d, fh), lambda ei, fi: (li, ei, 0, fi)),
            pl.BlockSpec((None, None, fh, d), lambda ei, fi: (li, ei, fi, 0)),
        ],
        out_specs=[out_spec, out_spec],
        out_shape=[out, out],
        scratch_shapes=[
            pltpu.VMEM((d, fh), BF16),
            pltpu.VMEM((d, fh), BF16),
            pltpu.VMEM((fh, d), BF16),
            pltpu.VMEM((batch * cap, d), F32),
        ],
        compiler_params=_cparams(("arbitrary", "arbitrary")),
        name="expert_ffn",
    )(xg, gc, w_gate_all, w_up_all, w_down_all)


def _combine_kernel(st_ref, nr_ref, h_ref, ps_ref, p_ref, yhi_hbm, ylo_hbm, wp_ref, wgt_ref,
                    gple_ref, gfin_ref, hout_ref, slab_ref, acc_ref, wpb_ref, wgb_ref, sem,
                    *, final, w, wp, cap, batch, tiles_per_batch):
    i = pl.program_id(0)
    t, e = ps_ref.shape
    slot = i & 1

    def window_copies(tile, q, buf):
        b = tile // tiles_per_batch
        copies = []
        for ei in range(e):
            ws = _window_start(st_ref[tile * e + ei] + q * w, BF16_ROWS, cap, wp)
            row0 = pl.multiple_of((ei * batch + b) * cap + ws, BF16_ROWS)
            for part, src in enumerate((yhi_hbm, ylo_hbm)):
                copies.append(pltpu.make_async_copy(
                    src.at[pl.ds(row0, wp), :], slab_ref.at[buf, part, pl.ds(ei * wp, wp), :],
                    sem.at[buf]))
        return copies

    @pl.when(i == 0)
    def _():
        _cast_rows(wp_ref, wpb_ref)
        _cast_rows(wgt_ref, wgb_ref)
        for c in window_copies(0, 0, 0):
            c.start()

    @pl.when(i + 1 < pl.num_programs(0))
    def _():
        for c in window_copies(i + 1, 0, 1 - slot):
            c.start()

    ps = ps_ref[...]
    pos = ps >> 1
    sel = (ps & 1) > 0
    start = pos[0:1, :]
    k = e * wp
    wp_shift = wp.bit_length() - 1
    expand = _one_hot(lax.broadcasted_iota(I32, (e, k), 1) >> wp_shift
                      == lax.broadcasted_iota(I32, (e, k), 0))
    lane_w = (lax.broadcasted_iota(I32, (t, k), 1) & (wp - 1)).astype(F32)

    def apply_round(q):
        lo = start + q * w
        ok = sel & (pos >= lo) & (pos < lo + w)
        prel = jnp.where(ok, pos - _window_start(lo, BF16_ROWS, cap, wp), -1)
        pexp = jnp.dot(prel.astype(F32).astype(BF16), expand, preferred_element_type=F32)
        oh = _one_hot(pexp == lane_w)
        acc_ref[...] += (jnp.dot(oh, slab_ref[slot, 0], preferred_element_type=F32)
                         + jnp.dot(oh, slab_ref[slot, 1], preferred_element_type=F32))

    acc_ref[...] = h_ref[...]
    for c in window_copies(i, 0, slot):
        c.wait()
    apply_round(0)

    def extra(q, _):
        for c in window_copies(i, q, slot):
            c.start()
        for c in window_copies(i, q, slot):
            c.wait()
        apply_round(q)
        return 0

    lax.fori_loop(1, nr_ref[i], extra, 0)

    hn = acc_ref[...]
    gate = _sigmoid(jnp.dot(_rms(hn, gple_ref[...]).astype(BF16), wgb_ref[...],
                            preferred_element_type=F32))
    hn = hn + gate * jnp.dot(p_ref[...].astype(BF16), wpb_ref[...], preferred_element_type=F32)
    if final:
        hn = _rms(hn, gfin_ref[...])
    hout_ref[...] = hn


def _combine(st_flat, nr, h, psel_tm, p_all, yhi, ylo, w_proj_all, w_gate_all, g_ple_all, li, g_final,
             batch, cap, tt, w, wp, final):
    n, d = h.shape
    e = psel_tm.shape[1]
    pd = p_all.shape[2]
    nt = n // tt
    grid_spec = pltpu.PrefetchScalarGridSpec(
        num_scalar_prefetch=2,
        grid=(nt,),
        in_specs=[
            pl.BlockSpec((tt, d), lambda i, st, nr: (i, 0)),
            pl.BlockSpec((tt, e), lambda i, st, nr: (i, 0)),
            pl.BlockSpec((None, tt, pd), lambda i, st, nr: (li, i, 0)),
            pl.BlockSpec(memory_space=pl.ANY),
            pl.BlockSpec(memory_space=pl.ANY),
            pl.BlockSpec((None, pd, d), lambda i, st, nr: (li, 0, 0)),
            pl.BlockSpec((None, d, d), lambda i, st, nr: (li, 0, 0)),
            pl.BlockSpec((None, 1, d), lambda i, st, nr: (li, 0, 0)),
            pl.BlockSpec((1, d), lambda i, st, nr: (0, 0)),
        ],
        out_specs=pl.BlockSpec((tt, d), lambda i, st, nr: (i, 0)),
        scratch_shapes=[
            pltpu.VMEM((2, 2, e * wp, d), BF16),
            pltpu.VMEM((tt, d), F32),
            pltpu.VMEM((pd, d), BF16),
            pltpu.VMEM((d, d), BF16),
            pltpu.SemaphoreType.DMA((2,)),
        ],
    )
    return pl.pallas_call(
        functools.partial(_combine_kernel, final=final, w=w, wp=wp, cap=cap, batch=batch,
                          tiles_per_batch=nt // batch),
        grid_spec=grid_spec,
        out_shape=jax.ShapeDtypeStruct((n, d), F32),
        compiler_params=_cparams(("arbitrary",)),
        name="combine_ple",
    )(st_flat, nr, h, psel_tm, p_all, yhi, ylo, w_proj_all, w_gate_all, g_ple_all,
      g_final.reshape(1, d))


def _fourier_in_kernel(h_ref, g_ref, w_ref, cs_ref, out_ref, wbf_ref, *, groups):
    @pl.when((pl.program_id(0) == 0) & (pl.program_id(1) == 0))
    def _():
        _cast_rows(w_ref, wbf_ref)

    xn = _rms(h_ref[...], g_ref[...])
    u = jnp.dot(xn.astype(BF16), wbf_ref[...], preferred_element_type=F32).astype(BF16)
    gd = u.shape[1] // groups
    for gi in range(groups):
        r = jnp.dot(u[:, gi * gd:(gi + 1) * gd], cs_ref[...], preferred_element_type=F32)
        out_ref[0, :, gi * gd:(gi + 1) * gd] = r[:, :gd].astype(BF16)
        out_ref[1, :, gi * gd:(gi + 1) * gd] = r[:, gd:].astype(BF16)


def _fourier_in(h, g_all, li, w_in_all, lj, cs_tab, batch, tm):
    n, d = h.shape
    s_len = n // batch
    nt = s_len // tm
    return pl.pallas_call(
        functools.partial(_fourier_in_kernel, groups=FOURIER_GROUPS),
        grid=(batch, nt),
        in_specs=[
            pl.BlockSpec((tm, d), lambda b, i: (b * nt + i, 0)),
            _layer_spec((1, d), li, 2),
            _layer_spec((d, d), lj, 2),
            pl.BlockSpec(cs_tab.shape, lambda b, i: (0, 0)),
        ],
        out_specs=pl.BlockSpec((None, 2, tm, d), lambda b, i: (b, 0, i, 0)),
        out_shape=jax.ShapeDtypeStruct((batch, 2, s_len, d), BF16),
        scratch_shapes=[pltpu.VMEM((d, d), BF16)],
        compiler_params=_cparams(("arbitrary", "arbitrary")),
        name="fourier_in",
    )(h, g_all, w_in_all, cs_tab)


def _seq_dft_kernel(tab_ref, rhs_ref, out_ref, acc_ref, *, scale):
    k = pl.program_id(2)

    @pl.when(k == 0)
    def _():
        acc_ref[...] = jnp.zeros(acc_ref.shape, F32)

    acc_ref[...] += jnp.dot(tab_ref[...], rhs_ref[...], preferred_element_type=F32)

    @pl.when(k == pl.num_programs(2) - 1)
    def _():
        out_ref[...] = (acc_ref[...] * scale).astype(out_ref.dtype)


def _seq_dft(ucs, tab, scale, tm, tk):
    batch, two_s, d = ucs.shape
    s_len = two_s // 2
    return pl.pallas_call(
        functools.partial(_seq_dft_kernel, scale=scale),
        grid=(batch, s_len // tm, two_s // tk),
        in_specs=[
            pl.BlockSpec((tm, tk), lambda b, i, k: (i, k)),
            pl.BlockSpec((None, tk, d), lambda b, i, k: (b, k, 0)),
        ],
        out_specs=pl.BlockSpec((None, tm, d), lambda b, i, k: (b, i, 0)),
        out_shape=jax.ShapeDtypeStruct((batch, s_len, d), BF16),
        scratch_shapes=[pltpu.VMEM((tm, d), F32)],
        compiler_params=_cparams(("arbitrary", "arbitrary", "arbitrary")),
        name="seq_dft",
    )(tab, ucs)


def _dft_tables(s_len, gd):
    def cs(n):
        i = np.arange(n, dtype=np.int64)
        ang = ((i[:, None] * i[None, :]) % n).astype(np.float64) * (2.0 * np.pi / n)
        return np.cos(ang), np.sin(ang)

    c_s, s_s = cs(s_len)
    c_c, s_c = cs(gd)
    return (jnp.asarray(np.concatenate([c_s, -s_s], axis=1), dtype=BF16),
            jnp.asarray(np.concatenate([c_c, s_c], axis=1), dtype=BF16))


def _window_rows(tt, cap, s_len):
    wp = min(max(2 * tt * cap // s_len, 2 * BF16_ROWS), cap)
    assert wp & (wp - 1) == 0
    return wp


def _tile_tables(psel, cap, tt, w):
    batch, e, s_len = psel.shape
    st = (psel >> 1)[:, :, ::tt]
    nxt = jnp.concatenate([st[:, :, 1:], jnp.full((batch, e, 1), cap, I32)], axis=2)
    most = jnp.max(nxt - st, axis=1).reshape(-1)
    return jnp.swapaxes(st, 1, 2).reshape(-1), jnp.maximum((most + (w - 1)) // w, 1)


def kernel(x, p, g_mix, g_ffn, g_ple, g_final, rg_w_in, rg_b_in, rg_conv_w, rg_conv_b,
           rg_gx_w, rg_gx_b, rg_ga_w, rg_ga_b, rg_lam, rg_w_out, rg_b_out,
           ft_w_in, ft_w_out, w_router, w_gate, w_up, w_down, ple_w_proj, ple_w_gate):
    batch, s_len, d = x.shape
    n = batch * s_len
    depth = g_mix.shape[0]
    n_exp = w_router.shape[2]
    cap = max(1, CAPACITY_FACTOR * s_len // n_exp)
    gd = d // FOURIER_GROUPS
    tm = min(512, s_len)
    chunk = min(512, s_len)
    tt_d, tt_c = min(256, s_len), min(512, s_len)
    wp_d, wp_c = _window_rows(tt_d, cap, s_len), _window_rows(tt_c, cap, s_len)
    w_d, w_c = wp_d - BF16_ROWS, wp_c - BF16_ROWS

    h = x.reshape(n, d)
    seq_tab, ch_tab = _dft_tables(s_len, gd)
    dft_scale = 1.0 / math.sqrt(s_len * gd)
    row = lambda a: a.reshape(a.shape[0], 1, a.shape[-1])
    g_mix3, g_ffn3, g_ple3 = row(g_mix), row(g_ffn), row(g_ple)
    w_router_t = jnp.swapaxes(w_router, 1, 2)
    p3 = p.reshape(depth, n, p.shape[-1])
    rg_b_in3, rg_conv_b3, rg_b_out3 = row(rg_b_in), row(rg_conv_b), row(rg_b_out)
    ft_b_out3 = jnp.zeros((ft_w_out.shape[0], 1, d), F32)

    for i in range(depth):
        j = i // 2
        if i % 2 == 0:
            gg, u = _rg_in(h, g_mix3, i, rg_w_in, rg_b_in3, j, tm)
            gx, ga = rg_gx_w[j], rg_ga_w[j]
            wg_cat = jnp.concatenate([gx[0], ga[0], gx[1], ga[1]], axis=-1).astype(BF16)
            gxb, gab = rg_gx_b[j], rg_ga_b[j]
            bg_cat = jnp.concatenate([gxb[0], gab[0], gxb[1], gab[1]], axis=-1)[:, None, :]
            y = _rg_scan(u, gg, rg_conv_w, rg_conv_b3, wg_cat, bg_cat, rg_lam, j, batch, chunk)
            h, xn, lt = _mix_out(y, h, rg_w_out, j, rg_b_out3, g_ffn3, i, w_router_t, tm)
        else:
            ucs = _fourier_in(h, g_mix3, i, ft_w_in, j, ch_tab, batch, tm)
            f = _seq_dft(ucs.reshape(batch, 2 * s_len, d), seq_tab, dft_scale,
                         min(1024, s_len), min(1024, s_len))
            h, xn, lt = _mix_out(f.reshape(n, d), h, ft_w_out, j, ft_b_out3, g_ffn3, i,
                                 w_router_t, tm)

        psel, gc = _route(lt, batch, cap)
        st_d, nr_d = _tile_tables(psel, cap, tt_d, w_d)
        st_c, nr_c = _tile_tables(psel, cap, tt_c, w_c)
        gc = jnp.swapaxes(gc[:, :, :cap // GATE_LANES, :].reshape(batch, n_exp, cap),
                          0, 1).reshape(n_exp, batch * cap, 1)
        psel_tm = jnp.swapaxes(psel, 1, 2).reshape(n, n_exp)
        xg = _dispatch(st_d, nr_d, xn, psel, batch, cap, tt_d, w_d, wp_d)
        yhi, ylo = _ffn(xg, gc, w_gate, w_up, w_down, i, 4)
        rows = n_exp * batch * cap
        h = _combine(st_c, nr_c, h, psel_tm, p3, yhi.reshape(rows, d), ylo.reshape(rows, d),
                     ple_w_proj, ple_w_gate, g_ple3, i, g_final, batch, cap, tt_c,
                     w_c, wp_c, final=(i == depth - 1))
    return h.reshape(batch, s_len, d)
```

```python
import functools
import math

import jax
import jax.numpy as jnp
import numpy as np
from jax import lax
from jax.experimental import pallas as pl
from jax.experimental.pallas import tpu as pltpu

EPS = 1e-6
RG_C = 8.0
FOURIER_GROUPS = 4
CAPACITY_FACTOR = 2

F32 = jnp.float32
BF16 = jnp.bfloat16
I32 = jnp.int32

SUBLANES = 8
BF16_ROWS = 16
GATE_LANES = 32
GATE_SHIFT = 5
VMEM_LIMIT = 56 * 1024 * 1024


def _cparams(sem):
    return pltpu.CompilerParams(dimension_semantics=sem, vmem_limit_bytes=VMEM_LIMIT)


def _layer_spec(tail, li, nargs):
    zeros = (0,) * len(tail)
    if nargs == 1:
        return pl.BlockSpec((None,) + tail, lambda i: (li,) + zeros)
    if nargs == 2:
        return pl.BlockSpec((None,) + tail, lambda i, j: (li,) + zeros)
    return pl.BlockSpec((None,) + tail, lambda i, j, k: (li,) + zeros)


def _rms(x, g):
    return x * lax.rsqrt(jnp.mean(x * x, axis=-1, keepdims=True) + EPS) * g


def _gelu_tanh(x):
    c = math.sqrt(2.0 / math.pi)
    return 0.5 * x * (1.0 + jnp.tanh(c * (x + 0.044715 * (x * x * x))))


def _sigmoid(x):
    return 1.0 / (1.0 + jnp.exp(-x))


def _one_hot(mask):
    return jnp.where(mask, 1.0, 0.0).astype(BF16)


def _cast_rows(src_ref, dst_ref, rows=256):
    n = src_ref.shape[0]
    rows = min(rows, n)
    assert n % rows == 0

    def body(i, _):
        r = pl.multiple_of(i * rows, rows)
        dst_ref[pl.ds(r, rows), :] = src_ref[pl.ds(r, rows), :].astype(dst_ref.dtype)
        return 0

    lax.fori_loop(0, n // rows, body, 0)


def _rg_in_kernel(h_ref, g_ref, w_ref, b_ref, gg_ref, u_ref, wbf_ref):
    @pl.when(pl.program_id(0) == 0)
    def _():
        _cast_rows(w_ref, wbf_ref)

    xn = _rms(h_ref[...], g_ref[...])
    proj = jnp.dot(xn.astype(BF16), wbf_ref[...], preferred_element_type=F32) + b_ref[...]
    w = gg_ref.shape[-1]
    gg_ref[...] = _gelu_tanh(proj[:, :w])
    u_ref[...] = proj[:, w:]


def _rg_in(h, g_all, li, w_in_all, b_in_all, lj, tm):
    n, d = h.shape
    w2 = w_in_all.shape[2]
    w = w2 // 2
    return pl.pallas_call(
        _rg_in_kernel,
        grid=(n // tm,),
        in_specs=[
            pl.BlockSpec((tm, d), lambda i: (i, 0)),
            _layer_spec((1, d), li, 1),
            _layer_spec((d, w2), lj, 1),
            _layer_spec((1, w2), lj, 1),
        ],
        out_specs=[
            pl.BlockSpec((tm, w), lambda i: (i, 0)),
            pl.BlockSpec((tm, w), lambda i: (i, 0)),
        ],
        out_shape=[jax.ShapeDtypeStruct((n, w), F32), jax.ShapeDtypeStruct((n, w), F32)],
        scratch_shapes=[pltpu.VMEM((d, w2), BF16)],
        compiler_params=_cparams(("arbitrary",)),
        name="rg_in",
    )(h, g_all, w_in_all, b_in_all)


def _tile_scan(a, b, reverse):
    t, c = a.shape
    a = a.reshape(t // SUBLANES, SUBLANES, c)
    b = b.reshape(t // SUBLANES, SUBLANES, c)
    row = lax.broadcasted_iota(I32, a.shape, 1)
    s = 1
    while s < SUBLANES:
        if reverse:
            m = row < (SUBLANES - s)
            shift = SUBLANES - s
        else:
            m = row >= s
            shift = s
        a_sh = jnp.where(m, pltpu.roll(a, shift, 1), 1.0)
        b_sh = jnp.where(m, pltpu.roll(b, shift, 1), 0.0)
        b = a * b_sh + b
        a = a * a_sh
        s *= 2
    return a.reshape(t, c), b.reshape(t, c)


def _rg_scan_kernel(u_ref, gg_ref, cw_ref, cb_ref, wg_ref, bg_ref, lam_ref, o_ref,
                    af_ref, bf_ref, ab_ref, bb_ref, *, chunk):
    s_len, c = u_ref.shape
    t = chunk
    n_chunks = s_len // t
    halo = SUBLANES
    lw = t + 2 * halo

    lam = -lam_ref[...]
    sp = jnp.maximum(lam, 0.0) + jnp.log1p(jnp.exp(-jnp.abs(lam)))
    cw = cw_ref[...]
    cb = cb_ref[...]
    bg = bg_ref[...]

    def pass1(ci, _):
        t0 = pl.multiple_of(ci * t, t)
        lo = pl.multiple_of(jnp.maximum(t0 - halo, 0), halo)
        hi = pl.multiple_of(jnp.minimum(t0 + t, s_len - halo), halo)
        prev = jnp.where(ci > 0, u_ref[pl.ds(lo, halo), :], 0.0)
        nxt = jnp.where(ci < n_chunks - 1, u_ref[pl.ds(hi, halo), :], 0.0)
        w = jnp.concatenate([prev, u_ref[pl.ds(t0, t), :], nxt], axis=0)
        conv = (cw[0:1] * pltpu.roll(w, 2, 0) + cw[1:2] * pltpu.roll(w, 1, 0)
                + cw[2:3] * w + cw[3:4] * pltpu.roll(w, lw - 1, 0))
        conv = conv[halo:halo + t] + cb
        gates = jnp.dot(conv.astype(BF16), wg_ref[...], preferred_element_type=F32) + bg
        for d, (a_ref, b_ref) in enumerate(((af_ref, bf_ref), (ab_ref, bb_ref))):
            i_t = _sigmoid(gates[:, (2 * d) * c:(2 * d + 1) * c])
            r_t = _sigmoid(gates[:, (2 * d + 1) * c:(2 * d + 2) * c])
            log_a = (-RG_C) * r_t * sp[d:d + 1]
            a_t = jnp.exp(log_a)
            b_t = jnp.sqrt(-jnp.tanh(log_a) * (a_t * a_t + 1.0)) * (i_t * conv)
            a_s, b_s = _tile_scan(a_t, b_t, reverse=(d == 1))
            a_ref[pl.ds(t0, t), :] = a_s
            b_ref[pl.ds(t0, t), :] = b_s
        return 0

    lax.fori_loop(0, n_chunks, pass1, 0)

    n_tiles = s_len // SUBLANES

    def pass2(k, carry):
        hf, hb = carry
        rf = pl.multiple_of(k * SUBLANES, SUBLANES)
        rb = pl.multiple_of((n_tiles - 1 - k) * SUBLANES, SUBLANES)
        h_f = bf_ref[pl.ds(rf, SUBLANES), :] + af_ref[pl.ds(rf, SUBLANES), :] * hf
        bf_ref[pl.ds(rf, SUBLANES), :] = h_f
        h_b = bb_ref[pl.ds(rb, SUBLANES), :] + ab_ref[pl.ds(rb, SUBLANES), :] * hb
        bb_ref[pl.ds(rb, SUBLANES), :] = h_b
        return h_f[SUBLANES - 1:SUBLANES, :], h_b[0:1, :]

    zero = jnp.zeros((1, c), F32)
    lax.fori_loop(0, n_tiles, pass2, (zero, zero), unroll=4)

    def pass3(ci, _):
        t0 = pl.multiple_of(ci * t, t)
        o_ref[pl.ds(t0, t), :] = ((bf_ref[pl.ds(t0, t), :] + bb_ref[pl.ds(t0, t), :])
                                  * gg_ref[pl.ds(t0, t), :]).astype(o_ref.dtype)
        return 0

    lax.fori_loop(0, n_chunks, pass3, 0)


def _rg_scan(u, gg, conv_w_all, conv_b_all, wg_cat, bg_cat, lam_all, lj, batch, chunk):
    n, w = u.shape
    s_len = n // batch
    heads, c, c4 = wg_cat.shape
    kw = conv_w_all.shape[1]
    u3 = u.reshape(batch, s_len, w)
    gg3 = gg.reshape(batch, s_len, w)
    out = pl.pallas_call(
        functools.partial(_rg_scan_kernel, chunk=chunk),
        grid=(batch, heads),
        in_specs=[
            pl.BlockSpec((None, s_len, c), lambda b, h: (b, 0, h)),
            pl.BlockSpec((None, s_len, c), lambda b, h: (b, 0, h)),
            pl.BlockSpec((None, kw, c), lambda b, h: (lj, 0, h)),
            pl.BlockSpec((None, 1, c), lambda b, h: (lj, 0, h)),
            pl.BlockSpec((None, c, c4), lambda b, h: (h, 0, 0)),
            pl.BlockSpec((None, 1, c4), lambda b, h: (h, 0, 0)),
            pl.BlockSpec((None, 2, c), lambda b, h: (lj, 0, h)),
        ],
        out_specs=pl.BlockSpec((None, s_len, c), lambda b, h: (b, 0, h)),
        out_shape=jax.ShapeDtypeStruct((batch, s_len, w), BF16),
        scratch_shapes=[pltpu.VMEM((s_len, c), F32)] * 4,
        compiler_params=_cparams(("arbitrary", "arbitrary")),
        name="rg_scan",
    )(u3, gg3, conv_w_all, conv_b_all, wg_cat, bg_cat, lam_all)
    return out.reshape(n, w)


def _mix_out_kernel(y_ref, h_ref, w_ref, b_ref, g_ref, wr_ref, hout_ref, xn_ref, lt_ref, wbf_ref):
    @pl.when(pl.program_id(0) == 0)
    def _():
        _cast_rows(w_ref, wbf_ref)

    hn = (h_ref[...] + jnp.dot(y_ref[...].astype(BF16), wbf_ref[...], preferred_element_type=F32)
          + b_ref[...])
    hout_ref[...] = hn
    xn = _rms(hn, g_ref[...])
    xn_ref[...] = xn.astype(BF16)
    lt_ref[...] = lax.dot_general(wr_ref[...], xn, (((1,), (1,)), ((), ())),
                                  precision=lax.Precision.HIGHEST, preferred_element_type=F32)


def _mix_out(y, h, w_all, lj, b_all, g_all, li, w_router_t_all, tm):
    n, d = h.shape
    k = y.shape[1]
    e = w_router_t_all.shape[1]
    return pl.pallas_call(
        _mix_out_kernel,
        grid=(n // tm,),
        in_specs=[
            pl.BlockSpec((tm, k), lambda i: (i, 0)),
            pl.BlockSpec((tm, d), lambda i: (i, 0)),
            _layer_spec((k, d), lj, 1),
            _layer_spec((1, d), lj, 1),
            _layer_spec((1, d), li, 1),
            _layer_spec((e, d), li, 1),
        ],
        out_specs=[
            pl.BlockSpec((tm, d), lambda i: (i, 0)),
            pl.BlockSpec((tm, d), lambda i: (i, 0)),
            pl.BlockSpec((e, tm), lambda i: (0, i)),
        ],
        out_shape=[jax.ShapeDtypeStruct((n, d), F32), jax.ShapeDtypeStruct((n, d), BF16),
                   jax.ShapeDtypeStruct((e, n), F32)],
        scratch_shapes=[pltpu.VMEM((k, d), BF16)],
        compiler_params=_cparams(("arbitrary",)),
        name="mix_out",
    )(y, h, w_all, b_all, g_all, w_router_t_all)


def _excl_cumsum_lanes(x):
    s_len = x.shape[1]
    lane = lax.broadcasted_iota(I32, x.shape, 1)
    acc = x
    s = 1
    while s < s_len:
        acc = acc + jnp.where(lane >= s, pltpu.roll(acc, s, 1), 0)
        s *= 2
    return acc - x


def _route_kernel(lt_ref, ps_ref, gc_ref, *, cap):
    e, s_len = lt_ref.shape
    lg = lt_ref[...]
    ex = jnp.exp(lg - jnp.max(lg, axis=0, keepdims=True))
    aff = ex / jnp.sum(ex, axis=0, keepdims=True)

    def count(mask):
        return jnp.sum(mask.astype(F32), axis=1, keepdims=True).astype(I32)

    def radix(i, prefix):
        cand = prefix | lax.shift_left(jnp.int32(1), 30 - i)
        cand_f = lax.bitcast_convert_type(cand, F32)
        return jnp.where(count(aff >= cand_f) >= cap, cand, prefix)

    thr_bits = lax.fori_loop(0, 31, radix, jnp.zeros((e, 1), I32))
    thr = lax.bitcast_convert_type(thr_bits, F32)
    above = lax.bitcast_convert_type(thr_bits + 1, F32)
    gt = aff >= above
    eq = (aff >= thr) & jnp.logical_not(gt)
    need = cap - count(gt)
    eq_rank = _excl_cumsum_lanes(eq.astype(I32))
    sel_i = (gt | (eq & (eq_rank < need))).astype(I32)
    pos = _excl_cumsum_lanes(sel_i)
    ps_ref[...] = pos * 2 + sel_i

    rh = gc_ref.shape[1]
    row_hi = lax.broadcasted_iota(I32, (rh, s_len), 0)
    row_lo = lax.broadcasted_iota(I32, (GATE_LANES, s_len), 0)
    a1 = aff.astype(BF16).astype(F32)
    a2 = (aff - a1).astype(BF16).astype(F32)
    a3 = (aff - a1 - a2).astype(BF16).astype(F32)
    for ei in range(e):
        pe = pos[ei:ei + 1, :]
        oh_hi = ((pe >> GATE_SHIFT) == row_hi) & (sel_i[ei:ei + 1, :] > 0)
        oh_lo = _one_hot((pe & (GATE_LANES - 1)) == row_lo)
        lhs = jnp.concatenate([jnp.where(oh_hi, a[ei:ei + 1, :], 0.0) for a in (a1, a2, a3)],
                              axis=0).astype(BF16)
        r = lax.dot_general(lhs, oh_lo, (((1,), (1,)), ((), ())), preferred_element_type=F32)
        gc_ref[ei] = r[:rh] + r[rh:2 * rh] + r[2 * rh:]


def _route(lt, batch, cap):
    e, n = lt.shape
    s_len = n // batch
    assert cap % GATE_LANES == 0
    rh = max(cap // GATE_LANES, BF16_ROWS)
    return pl.pallas_call(
        functools.partial(_route_kernel, cap=cap),
        grid=(batch,),
        in_specs=[pl.BlockSpec((e, s_len), lambda b: (0, b))],
        out_specs=[
            pl.BlockSpec((None, e, s_len), lambda b: (b, 0, 0)),
            pl.BlockSpec((None, e, rh, GATE_LANES), lambda b: (b, 0, 0, 0)),
        ],
        out_shape=[
            jax.ShapeDtypeStruct((batch, e, s_len), I32),
            jax.ShapeDtypeStruct((batch, e, rh, GATE_LANES), F32),
        ],
        compiler_params=_cparams(("arbitrary",)),
        name="route",
    )(lt)


def _window_start(first_slot, align, cap, wp):
    return jnp.minimum(first_slot & (-align), cap - wp)


def _dispatch_kernel(st_ref, nr_ref, xn_ref, ps_ref, out_ref, *, w, wp):
    b = pl.program_id(0)
    i = pl.program_id(1)
    tile = b * pl.num_programs(1) + i
    e, cap, d = out_ref.shape
    t = xn_ref.shape[0]

    @pl.when(i == 0)
    def _():
        def zero(ei, _):
            out_ref[ei] = jnp.zeros((cap, d), BF16)
            return 0

        lax.fori_loop(0, e, zero, 0)

    x = xn_ref[...]
    ps = ps_ref[...]
    pos = ps >> 1
    sel = (ps & 1) > 0
    start = pos[:, 0:1]
    sub = lax.broadcasted_iota(I32, (wp, t), 0)

    def one_round(q):
        lo = start + q * w
        ok = sel & (pos >= lo) & (pos < lo + w)
        prel = jnp.where(ok, pos - _window_start(lo, BF16_ROWS, cap, wp), -1)
        oh = jnp.concatenate([_one_hot(sub == prel[ei:ei + 1, :]) for ei in range(e)], axis=0)
        win = jnp.dot(oh, x, preferred_element_type=F32)
        for ei in range(e):
            ws = _window_start(st_ref[tile * e + ei] + q * w, BF16_ROWS, cap, wp)
            rows = pl.ds(pl.multiple_of(ws, BF16_ROWS), wp)
            out_ref[ei, rows, :] = out_ref[ei, rows, :] + win[ei * wp:(ei + 1) * wp].astype(BF16)

    one_round(0)

    def extra(q, _):
        one_round(q)
        return 0

    lax.fori_loop(1, nr_ref[tile], extra, 0)


def _dispatch(st_flat, nr, xn, psel, batch, cap, tt, w, wp):
    n, d = xn.shape
    s_len = n // batch
    e = psel.shape[1]
    nt = s_len // tt
    grid_spec = pltpu.PrefetchScalarGridSpec(
        num_scalar_prefetch=2,
        grid=(batch, nt),
        in_specs=[
            pl.BlockSpec((tt, d), lambda b, i, st, nr: (b * nt + i, 0)),
            pl.BlockSpec((None, e, tt), lambda b, i, st, nr: (b, 0, i)),
        ],
        out_specs=pl.BlockSpec((e, None, cap, d), lambda b, i, st, nr: (0, b, 0, 0)),
    )
    return pl.pallas_call(
        functools.partial(_dispatch_kernel, w=w, wp=wp),
        grid_spec=grid_spec,
        out_shape=jax.ShapeDtypeStruct((e, batch, cap, d), BF16),
        compiler_params=_cparams(("arbitrary", "arbitrary")),
        name="dispatch",
    )(st_flat, nr, xn, psel)


def _ffn_kernel(xg_ref, gc_ref, wg_ref, wu_ref, wd_ref, hi_ref, lo_ref,
                wgub_ref, wdb_ref, acc_ref):
    fh = pl.program_id(1)
    batch, cap, _ = xg_ref.shape

    fw = wg_ref.shape[1]
    _cast_rows(wg_ref, wgub_ref.at[:, pl.ds(0, fw)])
    _cast_rows(wu_ref, wgub_ref.at[:, pl.ds(fw, fw)])
    _cast_rows(wd_ref, wdb_ref)

    nb = 2 if batch % 2 == 0 else 1
    for b in range(0, batch, nb):
        x = xg_ref[pl.ds(b, nb)].reshape(nb * cap, xg_ref.shape[2])
        hgu = jnp.dot(x, wgub_ref[...], preferred_element_type=F32)
        hgate, hup = hgu[:, :fw], hgu[:, fw:]
        hg = (hgate * _sigmoid(hgate) * hup).astype(BF16)
        y = jnp.dot(hg, wdb_ref[...], preferred_element_type=F32)
        rows = pl.ds(b * cap, nb * cap)

        @pl.when(fh == 0)
        def _():
            acc_ref[rows, :] = y

        @pl.when(fh > 0)
        def _():
            acc_ref[rows, :] += y

    @pl.when(fh == pl.num_programs(1) - 1)
    def _():
        def split(i, _):
            rows = pl.ds(pl.multiple_of(i * cap, cap), cap)
            v = acc_ref[rows, :] * gc_ref[rows, :]
            hi = v.astype(BF16)
            hi_ref[rows, :] = hi
            lo_ref[rows, :] = (v - hi.astype(F32)).astype(BF16)
            return 0

        lax.fori_loop(0, batch, split, 0)


def _ffn(xg, gc, w_gate_all, w_up_all, w_down_all, li, n_fh):
    e, batch, cap, d = xg.shape
    f = w_gate_all.shape[3]
    fh = f // n_fh
    out = jax.ShapeDtypeStruct((e, batch * cap, d), BF16)
    out_spec = pl.BlockSpec((None, batch * cap, d), lambda ei, fi: (ei, 0, 0))
    return pl.pallas_call(
        _ffn_kernel,
        grid=(e, n_fh),
        in_specs=[
            pl.BlockSpec((None, batch, cap, d), lambda ei, fi: (ei, 0, 0, 0)),
            pl.BlockSpec((None, batch * cap, 1), lambda ei, fi: (ei, 0, 0)),
            pl.BlockSpec((None, None, d, fh), lambda ei, fi: (li, ei, 0, fi)),
            pl.BlockSpec((None, None, d, fh), lambda ei, fi: (li, ei, 0, fi)),
            pl.BlockSpec((None, None, fh, d), lambda ei, fi: (li, ei, fi, 0)),
        ],
        out_specs=[out_spec, out_spec],
        out_shape=[out, out],
        scratch_shapes=[
            pltpu.VMEM((d, 2 * fh), BF16),
            pltpu.VMEM((fh, d), BF16),
            pltpu.VMEM((batch * cap, d), F32),
        ],
        compiler_params=_cparams(("arbitrary", "arbitrary")),
        name="expert_ffn",
    )(xg, gc, w_gate_all, w_up_all, w_down_all)


def _combine_kernel(st_ref, nr_ref, h_ref, ps_ref, p_ref, yhi_hbm, ylo_hbm, wp_ref, wgt_ref,
                    gple_ref, gfin_ref, hout_ref, slab_ref, acc_ref, wpb_ref, wgb_ref, sem,
                    *, final, w, wp, cap, batch, tiles_per_batch):
    i = pl.program_id(0)
    t, e = ps_ref.shape
    slot = i & 1

    def window_copies(tile, q, buf):
        b = tile // tiles_per_batch
        copies = []
        for ei in range(e):
            ws = _window_start(st_ref[tile * e + ei] + q * w, BF16_ROWS, cap, wp)
            row0 = pl.multiple_of((ei * batch + b) * cap + ws, BF16_ROWS)
            for part, src in enumerate((yhi_hbm, ylo_hbm)):
                copies.append(pltpu.make_async_copy(
                    src.at[pl.ds(row0, wp), :], slab_ref.at[buf, part, pl.ds(ei * wp, wp), :],
                    sem.at[buf]))
        return copies

    @pl.when(i == 0)
    def _():
        _cast_rows(wp_ref, wpb_ref)
        _cast_rows(wgt_ref, wgb_ref)
        for c in window_copies(0, 0, 0):
            c.start()

    @pl.when(i + 1 < pl.num_programs(0))
    def _():
        for c in window_copies(i + 1, 0, 1 - slot):
            c.start()

    ps = ps_ref[...]
    pos = ps >> 1
    sel = (ps & 1) > 0
    start = pos[0:1, :]
    k = e * wp
    wp_shift = wp.bit_length() - 1
    expand = _one_hot(lax.broadcasted_iota(I32, (e, k), 1) >> wp_shift
                      == lax.broadcasted_iota(I32, (e, k), 0))
    lane_w = (lax.broadcasted_iota(I32, (t, k), 1) & (wp - 1)).astype(F32)

    def apply_round(q):
        lo = start + q * w
        ok = sel & (pos >= lo) & (pos < lo + w)
        prel = jnp.where(ok, pos - _window_start(lo, BF16_ROWS, cap, wp), -1)
        pexp = jnp.dot(prel.astype(F32).astype(BF16), expand, preferred_element_type=F32)
        oh = _one_hot(pexp == lane_w)
        acc_ref[...] += (jnp.dot(oh, slab_ref[slot, 0], preferred_element_type=F32)
                         + jnp.dot(oh, slab_ref[slot, 1], preferred_element_type=F32))

    acc_ref[...] = h_ref[...]
    for c in window_copies(i, 0, slot):
        c.wait()
    apply_round(0)

    def extra(q, _):
        for c in window_copies(i, q, slot):
            c.start()
        for c in window_copies(i, q, slot):
            c.wait()
        apply_round(q)
        return 0

    lax.fori_loop(1, nr_ref[i], extra, 0)

    hn = acc_ref[...]
    gate = _sigmoid(jnp.dot(_rms(hn, gple_ref[...]).astype(BF16), wgb_ref[...],
                            preferred_element_type=F32))
    hn = hn + gate * jnp.dot(p_ref[...].astype(BF16), wpb_ref[...], preferred_element_type=F32)
    if final:
        hn = _rms(hn, gfin_ref[...])
    hout_ref[...] = hn


def _combine(st_flat, nr, h, psel_tm, p_all, yhi, ylo, w_proj_all, w_gate_all, g_ple_all, li, g_final,
             batch, cap, tt, w, wp, final):
    n, d = h.shape
    e = psel_tm.shape[1]
    pd = p_all.shape[2]
    nt = n // tt
    grid_spec = pltpu.PrefetchScalarGridSpec(
        num_scalar_prefetch=2,
        grid=(nt,),
        in_specs=[
            pl.BlockSpec((tt, d), lambda i, st, nr: (i, 0)),
            pl.BlockSpec((tt, e), lambda i, st, nr: (i, 0)),
            pl.BlockSpec((None, tt, pd), lambda i, st, nr: (li, i, 0)),
            pl.BlockSpec(memory_space=pl.ANY),
            pl.BlockSpec(memory_space=pl.ANY),
            pl.BlockSpec((None, pd, d), lambda i, st, nr: (li, 0, 0)),
            pl.BlockSpec((None, d, d), lambda i, st, nr: (li, 0, 0)),
            pl.BlockSpec((None, 1, d), lambda i, st, nr: (li, 0, 0)),
            pl.BlockSpec((1, d), lambda i, st, nr: (0, 0)),
        ],
        out_specs=pl.BlockSpec((tt, d), lambda i, st, nr: (i, 0)),
        scratch_shapes=[
            pltpu.VMEM((2, 2, e * wp, d), BF16),
            pltpu.VMEM((tt, d), F32),
            pltpu.VMEM((pd, d), BF16),
            pltpu.VMEM((d, d), BF16),
            pltpu.SemaphoreType.DMA((2,)),
        ],
    )
    return pl.pallas_call(
        functools.partial(_combine_kernel, final=final, w=w, wp=wp, cap=cap, batch=batch,
                          tiles_per_batch=nt // batch),
        grid_spec=grid_spec,
        out_shape=jax.ShapeDtypeStruct((n, d), F32),
        compiler_params=_cparams(("arbitrary",)),
        name="combine_ple",
    )(st_flat, nr, h, psel_tm, p_all, yhi, ylo, w_proj_all, w_gate_all, g_ple_all,
      g_final.reshape(1, d))


RADIX = 16


def _fourier_in_kernel(h_ref, g_ref, w_ref, cs_ref, out_ref, wbf_ref, *, groups):
    @pl.when((pl.program_id(0) == 0) & (pl.program_id(1) == 0))
    def _():
        _cast_rows(w_ref, wbf_ref)

    n_a = h_ref.shape[0]
    d = h_ref.shape[1] // 2
    x = jnp.concatenate([h_ref[:, :d], h_ref[:, d:]], axis=0)
    xn = _rms(x, g_ref[...])
    u = jnp.dot(xn.astype(BF16), wbf_ref[...], preferred_element_type=F32).astype(BF16)
    gd = d // groups
    for gi in range(groups):
        cols = slice(gi * gd, (gi + 1) * gd)
        r = jnp.dot(u[:, cols], cs_ref[...], preferred_element_type=F32).astype(BF16)
        for q in range(2):
            out_ref[q, 0, :, cols] = r[q * n_a:(q + 1) * n_a, :gd]
            out_ref[q, 1, :, cols] = r[q * n_a:(q + 1) * n_a, gd:]


def _fourier_in(h, g_all, li, w_in_all, lj, cs_tab, batch):
    n, d = h.shape
    s_len = n // batch
    n_a = s_len // RADIX
    h3 = h.reshape(batch, n_a, RADIX * d)
    return pl.pallas_call(
        functools.partial(_fourier_in_kernel, groups=FOURIER_GROUPS),
        grid=(batch, RADIX // 2),
        in_specs=[
            pl.BlockSpec((None, n_a, 2 * d), lambda b, i: (b, 0, i)),
            _layer_spec((1, d), li, 2),
            _layer_spec((d, d), lj, 2),
            pl.BlockSpec(cs_tab.shape, lambda b, i: (0, 0)),
        ],
        out_specs=pl.BlockSpec((None, 2, 2, n_a, d), lambda b, i: (b, i, 0, 0, 0)),
        out_shape=jax.ShapeDtypeStruct((batch, RADIX, 2, n_a, d), BF16),
        scratch_shapes=[pltpu.VMEM((d, d), BF16)],
        compiler_params=_cparams(("arbitrary", "arbitrary")),
        name="fourier_in",
    )(h3, g_all, w_in_all, cs_tab)


def _seq_fft_kernel(y_ref, ta_ref, tb_ref, out_ref, scr_ref, *, scale):
    radix, _, n_a, dc = y_ref.shape
    g = BF16_ROWS
    n_m = n_a // g
    rg = radix * g

    def stage_a(n2, _):
        y = y_ref[n2].reshape(2 * n_a, dc)
        a = jnp.dot(ta_ref[n2], y, preferred_element_type=F32).astype(BF16)
        off = pl.multiple_of(n2 * g, g)
        for m in range(n_m):
            scr_ref[m, pl.ds(off, g), :] = a[m * g:(m + 1) * g]
            scr_ref[m, pl.ds(rg + off, g), :] = a[n_a + m * g:n_a + (m + 1) * g]
        return 0

    lax.fori_loop(0, radix, stage_a, 0)

    def stage_b(m, _):
        o = (jnp.dot(tb_ref[...], scr_ref[m], preferred_element_type=F32) * scale
             ).astype(out_ref.dtype)
        base = pl.multiple_of(m * g, g)
        for k2 in range(radix):
            out_ref[pl.ds(k2 * n_a + base, g), :] = o[k2 * g:(k2 + 1) * g]
        return 0

    lax.fori_loop(0, n_m, stage_b, 0)


def _seq_fft(ucs, tab_a, tab_b, scale, dc):
    batch, radix, _, n_a, d = ucs.shape
    s_len = radix * n_a
    assert n_a % BF16_ROWS == 0
    rg = radix * BF16_ROWS
    return pl.pallas_call(
        functools.partial(_seq_fft_kernel, scale=scale),
        grid=(batch, d // dc),
        in_specs=[
            pl.BlockSpec((None, radix, 2, n_a, dc), lambda b, j: (b, 0, 0, 0, j)),
            pl.BlockSpec(tab_a.shape, lambda b, j: (0, 0, 0)),
            pl.BlockSpec(tab_b.shape, lambda b, j: (0, 0)),
        ],
        out_specs=pl.BlockSpec((None, s_len, dc), lambda b, j: (b, 0, j)),
        out_shape=jax.ShapeDtypeStruct((batch, s_len, d), BF16),
        scratch_shapes=[pltpu.VMEM((n_a // BF16_ROWS, 2 * rg, dc), BF16)],
        compiler_params=_cparams(("arbitrary", "arbitrary")),
        name="seq_fft",
    )(ucs, tab_a, tab_b)


def _dft_tables(s_len, gd):
    n_a = s_len // RADIX
    i = np.arange(gd, dtype=np.int64)
    ang = ((i[:, None] * i[None, :]) % gd).astype(np.float64) * (2.0 * np.pi / gd)
    ch = np.concatenate([np.cos(ang), np.sin(ang)], axis=1)

    k1 = np.arange(n_a, dtype=np.int64)[None, :, None]
    n1 = np.arange(n_a, dtype=np.int64)[None, None, :]
    n2 = np.arange(RADIX, dtype=np.int64)[:, None, None]
    ang = ((k1 * (RADIX * n1 + n2)) % s_len).astype(np.float64) * (2.0 * np.pi / s_len)
    c, sn = np.cos(ang), np.sin(ang)
    tab_a = np.concatenate([np.concatenate([c, -sn], axis=2),
                            np.concatenate([-sn, -c], axis=2)], axis=1)

    r = np.arange(RADIX, dtype=np.int64)
    ang = ((r[:, None] * r[None, :]) % RADIX).astype(np.float64) * (2.0 * np.pi / RADIX)
    eye = np.eye(BF16_ROWS)
    tab_b = np.concatenate([np.kron(np.cos(ang), eye), np.kron(np.sin(ang), eye)], axis=1)
    as_bf16 = lambda a: jnp.asarray(a, dtype=F32).astype(BF16)
    return as_bf16(ch), as_bf16(tab_a), as_bf16(tab_b)


def _window_rows(tt, cap, s_len):
    wp = min(max(2 * tt * cap // s_len, 2 * BF16_ROWS), cap)
    assert wp & (wp - 1) == 0
    return wp


def _tile_tables(psel, cap, tt, w):
    batch, e, s_len = psel.shape
    st = (psel >> 1)[:, :, ::tt]
    nxt = jnp.concatenate([st[:, :, 1:], jnp.full((batch, e, 1), cap, I32)], axis=2)
    most = jnp.max(nxt - st, axis=1).reshape(-1)
    return jnp.swapaxes(st, 1, 2).reshape(-1), jnp.maximum((most + (w - 1)) // w, 1)


def kernel(x, p, g_mix, g_ffn, g_ple, g_final, rg_w_in, rg_b_in, rg_conv_w, rg_conv_b,
           rg_gx_w, rg_gx_b, rg_ga_w, rg_ga_b, rg_lam, rg_w_out, rg_b_out,
           ft_w_in, ft_w_out, w_router, w_gate, w_up, w_down, ple_w_proj, ple_w_gate):
    batch, s_len, d = x.shape
    n = batch * s_len
    depth = g_mix.shape[0]
    n_exp = w_router.shape[2]
    cap = max(1, CAPACITY_FACTOR * s_len // n_exp)
    gd = d // FOURIER_GROUPS
    tm = min(512, s_len)
    chunk = min(512, s_len)
    tt_d = tt_c = min(256, s_len)
    wp_d, wp_c = _window_rows(tt_d, cap, s_len), _window_rows(tt_c, cap, s_len)
    w_d, w_c = wp_d - BF16_ROWS, wp_c - BF16_ROWS

    h = x.reshape(n, d)
    ch_tab, fft_tab_a, fft_tab_b = _dft_tables(s_len, gd)
    dft_scale = 1.0 / math.sqrt(s_len * gd)
    row = lambda a: a.reshape(a.shape[0], 1, a.shape[-1])
    g_mix3, g_ffn3, g_ple3 = row(g_mix), row(g_ffn), row(g_ple)
    w_router_t = jnp.swapaxes(w_router, 1, 2)
    p3 = p.reshape(depth, n, p.shape[-1])
    rg_b_in3, rg_conv_b3, rg_b_out3 = row(rg_b_in), row(rg_conv_b), row(rg_b_out)
    ft_b_out3 = jnp.zeros((ft_w_out.shape[0], 1, d), F32)

    for i in range(depth):
        j = i // 2
        if i % 2 == 0:
            gg, u = _rg_in(h, g_mix3, i, rg_w_in, rg_b_in3, j, tm)
            gx, ga = rg_gx_w[j], rg_ga_w[j]
            wg_cat = jnp.concatenate([gx[0], ga[0], gx[1], ga[1]], axis=-1).astype(BF16)
            gxb, gab = rg_gx_b[j], rg_ga_b[j]
            bg_cat = jnp.concatenate([gxb[0], gab[0], gxb[1], gab[1]], axis=-1)[:, None, :]
            y = _rg_scan(u, gg, rg_conv_w, rg_conv_b3, wg_cat, bg_cat, rg_lam, j, batch, chunk)
            h, xn, lt = _mix_out(y, h, rg_w_out, j, rg_b_out3, g_ffn3, i, w_router_t, tm)
        else:
            ucs = _fourier_in(h, g_mix3, i, ft_w_in, j, ch_tab, batch)
            f = _seq_fft(ucs, fft_tab_a, fft_tab_b, dft_scale, min(256, d))
            h, xn, lt = _mix_out(f.reshape(n, d), h, ft_w_out, j, ft_b_out3, g_ffn3, i,
                                 w_router_t, tm)

        psel, gc = _route(lt, batch, cap)
        st_d, nr_d = _tile_tables(psel, cap, tt_d, w_d)
        st_c, nr_c = _tile_tables(psel, cap, tt_c, w_c)
        gc = jnp.swapaxes(gc[:, :, :cap // GATE_LANES, :].reshape(batch, n_exp, cap),
                          0, 1).reshape(n_exp, batch * cap, 1)
        psel_tm = jnp.swapaxes(psel, 1, 2).reshape(n, n_exp)
        xg = _dispatch(st_d, nr_d, xn, psel, batch, cap, tt_d, w_d, wp_d)
        yhi, ylo = _ffn(xg, gc, w_gate, w_up, w_down, i, 4)
        rows = n_exp * batch * cap
        h = _combine(st_c, nr_c, h, psel_tm, p3, yhi.reshape(rows, d), ylo.reshape(rows, d),
                     ple_w_proj, ple_w_gate, g_ple3, i, g_final, batch, cap, tt_c,
                     w_c, wp_c, final=(i == depth - 1))
    return h.reshape(batch, s_len, d)
```

```python
import functools
import math

import jax
import jax.numpy as jnp
import numpy as np
from jax import lax
from jax.experimental import pallas as pl
from jax.experimental.pallas import tpu as pltpu

EPS = 1e-6
RG_C = 8.0
FOURIER_GROUPS = 4
CAPACITY_FACTOR = 2

F32 = jnp.float32
BF16 = jnp.bfloat16
I32 = jnp.int32

SUBLANES = 8
BF16_ROWS = 16
GATE_LANES = 32
GATE_SHIFT = 5
VMEM_LIMIT = 56 * 1024 * 1024


def _cparams(sem):
    return pltpu.CompilerParams(dimension_semantics=sem, vmem_limit_bytes=VMEM_LIMIT)


def _layer_spec(tail, li, nargs):
    zeros = (0,) * len(tail)
    if nargs == 1:
        return pl.BlockSpec((None,) + tail, lambda i: (li,) + zeros)
    if nargs == 2:
        return pl.BlockSpec((None,) + tail, lambda i, j: (li,) + zeros)
    return pl.BlockSpec((None,) + tail, lambda i, j, k: (li,) + zeros)


def _rms(x, g):
    return x * lax.rsqrt(jnp.mean(x * x, axis=-1, keepdims=True) + EPS) * g


def _gelu_tanh(x):
    c = math.sqrt(2.0 / math.pi)
    return 0.5 * x * (1.0 + jnp.tanh(c * (x + 0.044715 * (x * x * x))))


def _sigmoid(x):
    return 1.0 / (1.0 + jnp.exp(-x))


def _one_hot(mask):
    return jnp.where(mask, 1.0, 0.0).astype(BF16)


def _cast_rows(src_ref, dst_ref, rows=256):
    n = src_ref.shape[0]
    rows = min(rows, n)
    assert n % rows == 0

    def body(i, _):
        r = pl.multiple_of(i * rows, rows)
        dst_ref[pl.ds(r, rows), :] = src_ref[pl.ds(r, rows), :].astype(dst_ref.dtype)
        return 0

    lax.fori_loop(0, n // rows, body, 0)


def _rg_in_kernel(h_ref, g_ref, w_ref, b_ref, gg_ref, u_ref, wbf_ref):
    @pl.when(pl.program_id(0) == 0)
    def _():
        _cast_rows(w_ref, wbf_ref)

    xn = _rms(h_ref[...], g_ref[...])
    proj = jnp.dot(xn.astype(BF16), wbf_ref[...], preferred_element_type=F32) + b_ref[...]
    w = gg_ref.shape[-1]
    gg_ref[...] = _gelu_tanh(proj[:, :w])
    u_ref[...] = proj[:, w:]


def _rg_in(h, g_all, li, w_in_all, b_in_all, lj, tm):
    n, d = h.shape
    w2 = w_in_all.shape[2]
    w = w2 // 2
    return pl.pallas_call(
        _rg_in_kernel,
        grid=(n // tm,),
        in_specs=[
            pl.BlockSpec((tm, d), lambda i: (i, 0)),
            _layer_spec((1, d), li, 1),
            _layer_spec((d, w2), lj, 1),
            _layer_spec((1, w2), lj, 1),
        ],
        out_specs=[
            pl.BlockSpec((tm, w), lambda i: (i, 0)),
            pl.BlockSpec((tm, w), lambda i: (i, 0)),
        ],
        out_shape=[jax.ShapeDtypeStruct((n, w), F32), jax.ShapeDtypeStruct((n, w), F32)],
        scratch_shapes=[pltpu.VMEM((d, w2), BF16)],
        compiler_params=_cparams(("arbitrary",)),
        name="rg_in",
    )(h, g_all, w_in_all, b_in_all)


def _tile_scan(a, b, reverse):
    t, c = a.shape
    a = a.reshape(t // SUBLANES, SUBLANES, c)
    b = b.reshape(t // SUBLANES, SUBLANES, c)
    row = lax.broadcasted_iota(I32, a.shape, 1)
    s = 1
    while s < SUBLANES:
        if reverse:
            m = row < (SUBLANES - s)
            shift = SUBLANES - s
        else:
            m = row >= s
            shift = s
        a_sh = jnp.where(m, pltpu.roll(a, shift, 1), 1.0)
        b_sh = jnp.where(m, pltpu.roll(b, shift, 1), 0.0)
        b = a * b_sh + b
        a = a * a_sh
        s *= 2
    return a.reshape(t, c), b.reshape(t, c)


def _rg_scan_kernel(u_ref, gg_ref, cw_ref, cb_ref, wg_ref, bg_ref, lam_ref, o_ref,
                    af_ref, bf_ref, ab_ref, bb_ref, *, chunk):
    s_len, c = u_ref.shape
    t = chunk
    n_chunks = s_len // t
    halo = SUBLANES
    lw = t + 2 * halo

    lam = -lam_ref[...]
    sp = jnp.maximum(lam, 0.0) + jnp.log1p(jnp.exp(-jnp.abs(lam)))
    cw = cw_ref[...]
    cb = cb_ref[...]
    bg = bg_ref[...]

    def pass1(ci, _):
        t0 = pl.multiple_of(ci * t, t)
        lo = pl.multiple_of(jnp.maximum(t0 - halo, 0), halo)
        hi = pl.multiple_of(jnp.minimum(t0 + t, s_len - halo), halo)
        prev = jnp.where(ci > 0, u_ref[pl.ds(lo, halo), :], 0.0)
        nxt = jnp.where(ci < n_chunks - 1, u_ref[pl.ds(hi, halo), :], 0.0)
        w = jnp.concatenate([prev, u_ref[pl.ds(t0, t), :], nxt], axis=0)
        conv = (cw[0:1] * pltpu.roll(w, 2, 0) + cw[1:2] * pltpu.roll(w, 1, 0)
                + cw[2:3] * w + cw[3:4] * pltpu.roll(w, lw - 1, 0))
        conv = conv[halo:halo + t] + cb
        gates = jnp.dot(conv.astype(BF16), wg_ref[...], preferred_element_type=F32) + bg
        for d, (a_ref, b_ref) in enumerate(((af_ref, bf_ref), (ab_ref, bb_ref))):
            i_t = _sigmoid(gates[:, (2 * d) * c:(2 * d + 1) * c])
            r_t = _sigmoid(gates[:, (2 * d + 1) * c:(2 * d + 2) * c])
            log_a = (-RG_C) * r_t * sp[d:d + 1]
            a_t = jnp.exp(log_a)
            b_t = jnp.sqrt(-jnp.tanh(log_a) * (a_t * a_t + 1.0)) * (i_t * conv)
            a_s, b_s = _tile_scan(a_t, b_t, reverse=(d == 1))
            a_ref[pl.ds(t0, t), :] = a_s
            b_ref[pl.ds(t0, t), :] = b_s
        return 0

    lax.fori_loop(0, n_chunks, pass1, 0)

    n_tiles = s_len // SUBLANES

    def pass2(k, carry):
        hf, hb = carry
        rf = pl.multiple_of(k * SUBLANES, SUBLANES)
        rb = pl.multiple_of((n_tiles - 1 - k) * SUBLANES, SUBLANES)
        h_f = bf_ref[pl.ds(rf, SUBLANES), :] + af_ref[pl.ds(rf, SUBLANES), :] * hf
        bf_ref[pl.ds(rf, SUBLANES), :] = h_f
        h_b = bb_ref[pl.ds(rb, SUBLANES), :] + ab_ref[pl.ds(rb, SUBLANES), :] * hb
        bb_ref[pl.ds(rb, SUBLANES), :] = h_b
        return h_f[SUBLANES - 1:SUBLANES, :], h_b[0:1, :]

    zero = jnp.zeros((1, c), F32)
    lax.fori_loop(0, n_tiles, pass2, (zero, zero), unroll=4)

    def pass3(ci, _):
        t0 = pl.multiple_of(ci * t, t)
        o_ref[pl.ds(t0, t), :] = ((bf_ref[pl.ds(t0, t), :] + bb_ref[pl.ds(t0, t), :])
                                  * gg_ref[pl.ds(t0, t), :]).astype(o_ref.dtype)
        return 0

    lax.fori_loop(0, n_chunks, pass3, 0)


def _rg_scan(u, gg, conv_w_all, conv_b_all, wg_cat, bg_cat, lam_all, lj, batch, chunk):
    n, w = u.shape
    s_len = n // batch
    heads, c, c4 = wg_cat.shape
    kw = conv_w_all.shape[1]
    u3 = u.reshape(batch, s_len, w)
    gg3 = gg.reshape(batch, s_len, w)
    out = pl.pallas_call(
        functools.partial(_rg_scan_kernel, chunk=chunk),
        grid=(batch, heads),
        in_specs=[
            pl.BlockSpec((None, s_len, c), lambda b, h: (b, 0, h)),
            pl.BlockSpec((None, s_len, c), lambda b, h: (b, 0, h)),
            pl.BlockSpec((None, kw, c), lambda b, h: (lj, 0, h)),
            pl.BlockSpec((None, 1, c), lambda b, h: (lj, 0, h)),
            pl.BlockSpec((None, c, c4), lambda b, h: (h, 0, 0)),
            pl.BlockSpec((None, 1, c4), lambda b, h: (h, 0, 0)),
            pl.BlockSpec((None, 2, c), lambda b, h: (lj, 0, h)),
        ],
        out_specs=pl.BlockSpec((None, s_len, c), lambda b, h: (b, 0, h)),
        out_shape=jax.ShapeDtypeStruct((batch, s_len, w), BF16),
        scratch_shapes=[pltpu.VMEM((s_len, c), F32)] * 4,
        compiler_params=_cparams(("arbitrary", "arbitrary")),
        name="rg_scan",
    )(u3, gg3, conv_w_all, conv_b_all, wg_cat, bg_cat, lam_all)
    return out.reshape(n, w)


def _mix_out_kernel(y_ref, h_ref, w_ref, b_ref, g_ref, wr_ref, hout_ref, xn_ref, lt_ref, wbf_ref):
    @pl.when(pl.program_id(0) == 0)
    def _():
        _cast_rows(w_ref, wbf_ref)

    hn = (h_ref[...] + jnp.dot(y_ref[...].astype(BF16), wbf_ref[...], preferred_element_type=F32)
          + b_ref[...])
    hout_ref[...] = hn
    xn = _rms(hn, g_ref[...])
    xn_ref[...] = xn.astype(BF16)
    lt_ref[...] = lax.dot_general(wr_ref[...], xn, (((1,), (1,)), ((), ())),
                                  precision=lax.Precision.HIGHEST, preferred_element_type=F32)


def _mix_out(y, h, w_all, lj, b_all, g_all, li, w_router_t_all, tm):
    n, d = h.shape
    k = y.shape[1]
    e = w_router_t_all.shape[1]
    return pl.pallas_call(
        _mix_out_kernel,
        grid=(n // tm,),
        in_specs=[
            pl.BlockSpec((tm, k), lambda i: (i, 0)),
            pl.BlockSpec((tm, d), lambda i: (i, 0)),
            _layer_spec((k, d), lj, 1),
            _layer_spec((1, d), lj, 1),
            _layer_spec((1, d), li, 1),
            _layer_spec((e, d), li, 1),
        ],
        out_specs=[
            pl.BlockSpec((tm, d), lambda i: (i, 0)),
            pl.BlockSpec((tm, d), lambda i: (i, 0)),
            pl.BlockSpec((e, tm), lambda i: (0, i)),
        ],
        out_shape=[jax.ShapeDtypeStruct((n, d), F32), jax.ShapeDtypeStruct((n, d), BF16),
                   jax.ShapeDtypeStruct((e, n), F32)],
        scratch_shapes=[pltpu.VMEM((k, d), BF16)],
        compiler_params=_cparams(("arbitrary",)),
        name="mix_out",
    )(y, h, w_all, b_all, g_all, w_router_t_all)


def _excl_cumsum_lanes(x):
    s_len = x.shape[1]
    lane = lax.broadcasted_iota(I32, x.shape, 1)
    acc = x
    s = 1
    while s < s_len:
        acc = acc + jnp.where(lane >= s, pltpu.roll(acc, s, 1), 0)
        s *= 2
    return acc - x


def _route_kernel(lt_ref, ps_ref, gc_ref, *, cap):
    e, s_len = lt_ref.shape
    lg = lt_ref[...]
    ex = jnp.exp(lg - jnp.max(lg, axis=0, keepdims=True))
    aff = ex / jnp.sum(ex, axis=0, keepdims=True)

    def count(mask):
        return jnp.sum(mask.astype(F32), axis=1, keepdims=True).astype(I32)

    def radix(i, prefix):
        cand = prefix | lax.shift_left(jnp.int32(1), 30 - i)
        cand_f = lax.bitcast_convert_type(cand, F32)
        return jnp.where(count(aff >= cand_f) >= cap, cand, prefix)

    thr_bits = lax.fori_loop(0, 31, radix, jnp.zeros((e, 1), I32))
    thr = lax.bitcast_convert_type(thr_bits, F32)
    above = lax.bitcast_convert_type(thr_bits + 1, F32)
    gt = aff >= above
    eq = (aff >= thr) & jnp.logical_not(gt)
    need = cap - count(gt)
    eq_rank = _excl_cumsum_lanes(eq.astype(I32))
    sel_i = (gt | (eq & (eq_rank < need))).astype(I32)
    pos = _excl_cumsum_lanes(sel_i)
    ps_ref[...] = pos * 2 + sel_i

    rh = max(cap // GATE_LANES, BF16_ROWS)
    row_hi = lax.broadcasted_iota(I32, (rh, s_len), 0)
    row_lo = lax.broadcasted_iota(I32, (GATE_LANES, s_len), 0)
    a1 = aff.astype(BF16).astype(F32)
    a2 = (aff - a1).astype(BF16).astype(F32)
    a3 = (aff - a1 - a2).astype(BF16).astype(F32)
    slot_r = lax.broadcasted_iota(I32, (cap, rh), 0)
    pick_hi = _one_hot((slot_r >> GATE_SHIFT) == lax.broadcasted_iota(I32, (cap, rh), 1))
    pick_lo = ((lax.broadcasted_iota(I32, (cap, GATE_LANES), 0) & (GATE_LANES - 1))
               == lax.broadcasted_iota(I32, (cap, GATE_LANES), 1))
    for ei in range(e):
        pe = pos[ei:ei + 1, :]
        oh_hi = ((pe >> GATE_SHIFT) == row_hi) & (sel_i[ei:ei + 1, :] > 0)
        oh_lo = _one_hot((pe & (GATE_LANES - 1)) == row_lo)
        lhs = jnp.concatenate([jnp.where(oh_hi, a[ei:ei + 1, :], 0.0) for a in (a1, a2, a3)],
                              axis=0).astype(BF16)
        r = lax.dot_general(lhs, oh_lo, (((1,), (1,)), ((), ())), preferred_element_type=F32)
        r = r.astype(BF16)
        rows = sum(jnp.dot(pick_hi, r[k * rh:(k + 1) * rh], preferred_element_type=F32)
                   for k in range(3))
        gc_ref[ei] = jnp.sum(jnp.where(pick_lo, rows, 0.0), axis=1, keepdims=True)


def _route(lt, batch, cap):
    e, n = lt.shape
    s_len = n // batch
    assert cap % GATE_LANES == 0
    return pl.pallas_call(
        functools.partial(_route_kernel, cap=cap),
        grid=(batch,),
        in_specs=[pl.BlockSpec((e, s_len), lambda b: (0, b))],
        out_specs=[
            pl.BlockSpec((None, e, s_len), lambda b: (b, 0, 0)),
            pl.BlockSpec((None, e, cap, 1), lambda b: (b, 0, 0, 0)),
        ],
        out_shape=[
            jax.ShapeDtypeStruct((batch, e, s_len), I32),
            jax.ShapeDtypeStruct((batch, e, cap, 1), F32),
        ],
        compiler_params=_cparams(("arbitrary",)),
        name="route",
    )(lt)


def _window(start, q, wp, cap):
    u = (start & (-BF16_ROWS)) + q * wp
    return u, jnp.minimum(u, cap - wp)


def _dispatch_kernel(st_ref, nr_ref, xn_ref, ps_ref, out_ref, *, wp):
    b = pl.program_id(0)
    i = pl.program_id(1)
    tile = b * pl.num_programs(1) + i
    e, cap, d = out_ref.shape
    t = xn_ref.shape[0]

    @pl.when(i == 0)
    def _():
        def zero(ei, _):
            out_ref[ei] = jnp.zeros((cap, d), BF16)
            return 0

        lax.fori_loop(0, e, zero, 0)

    x = xn_ref[...]
    ps = ps_ref[...]
    pos = ps >> 1
    sel = (ps & 1) > 0
    start = pos[:, 0:1]
    sub = lax.broadcasted_iota(I32, (wp, t), 0)

    def one_round(q):
        u, c = _window(start, q, wp, cap)
        prel = jnp.where(sel & (pos >= u) & (pos < u + wp), pos - c, -1)
        oh = jnp.concatenate([_one_hot(sub == prel[ei:ei + 1, :]) for ei in range(e)], axis=0)
        win = jnp.dot(oh, x, preferred_element_type=F32)
        for ei in range(e):
            ws = _window(st_ref[tile * e + ei], q, wp, cap)[1]
            rows = pl.ds(pl.multiple_of(ws, BF16_ROWS), wp)
            out_ref[ei, rows, :] = out_ref[ei, rows, :] + win[ei * wp:(ei + 1) * wp].astype(BF16)

    one_round(0)

    def extra(q, _):
        one_round(q)
        return 0

    lax.fori_loop(1, nr_ref[tile], extra, 0)


def _dispatch(st_flat, nr, xn, psel, batch, cap, tt, wp):
    n, d = xn.shape
    s_len = n // batch
    e = psel.shape[1]
    nt = s_len // tt
    grid_spec = pltpu.PrefetchScalarGridSpec(
        num_scalar_prefetch=2,
        grid=(batch, nt),
        in_specs=[
            pl.BlockSpec((tt, d), lambda b, i, st, nr: (b * nt + i, 0)),
            pl.BlockSpec((None, e, tt), lambda b, i, st, nr: (b, 0, i)),
        ],
        out_specs=pl.BlockSpec((e, None, cap, d), lambda b, i, st, nr: (0, b, 0, 0)),
    )
    return pl.pallas_call(
        functools.partial(_dispatch_kernel, wp=wp),
        grid_spec=grid_spec,
        out_shape=jax.ShapeDtypeStruct((e, batch, cap, d), BF16),
        compiler_params=_cparams(("arbitrary", "arbitrary")),
        name="dispatch",
    )(st_flat, nr, xn, psel)


def _ffn_kernel(xg_ref, gc_ref, wg_ref, wu_ref, wd_ref, hi_ref, lo_ref,
                wgub_ref, wdb_ref, acc_ref):
    fh = pl.program_id(1)
    batch, cap, _ = xg_ref.shape

    fw = wg_ref.shape[1]
    _cast_rows(wg_ref, wgub_ref.at[:, pl.ds(0, fw)])
    _cast_rows(wu_ref, wgub_ref.at[:, pl.ds(fw, fw)])
    _cast_rows(wd_ref, wdb_ref)

    nb = 2 if batch % 2 == 0 else 1
    for b in range(0, batch, nb):
        x = xg_ref[pl.ds(b, nb)].reshape(nb * cap, xg_ref.shape[2])
        hgu = jnp.dot(x, wgub_ref[...], preferred_element_type=F32)
        hgate, hup = hgu[:, :fw], hgu[:, fw:]
        hg = (hgate * _sigmoid(hgate) * hup).astype(BF16)
        y = jnp.dot(hg, wdb_ref[...], preferred_element_type=F32)
        rows = pl.ds(b * cap, nb * cap)

        @pl.when(fh == 0)
        def _():
            acc_ref[rows, :] = y

        @pl.when(fh > 0)
        def _():
            acc_ref[rows, :] += y

    @pl.when(fh == pl.num_programs(1) - 1)
    def _():
        def split(i, _):
            rows = pl.ds(pl.multiple_of(i * cap, cap), cap)
            v = acc_ref[rows, :] * gc_ref[i]
            hi = v.astype(BF16)
            hi_ref[rows, :] = hi
            lo_ref[rows, :] = (v - hi.astype(F32)).astype(BF16)
            return 0

        lax.fori_loop(0, batch, split, 0)


def _ffn(xg, gc, w_gate_all, w_up_all, w_down_all, li, n_fh):
    e, batch, cap, d = xg.shape
    f = w_gate_all.shape[3]
    fh = f // n_fh
    out = jax.ShapeDtypeStruct((e, batch * cap, d), BF16)
    out_spec = pl.BlockSpec((None, batch * cap, d), lambda ei, fi: (ei, 0, 0))
    return pl.pallas_call(
        _ffn_kernel,
        grid=(e, n_fh),
        in_specs=[
            pl.BlockSpec((None, batch, cap, d), lambda ei, fi: (ei, 0, 0, 0)),
            pl.BlockSpec((batch, None, cap, 1), lambda ei, fi: (0, ei, 0, 0)),
            pl.BlockSpec((None, None, d, fh), lambda ei, fi: (li, ei, 0, fi)),
            pl.BlockSpec((None, None, d, fh), lambda ei, fi: (li, ei, 0, fi)),
            pl.BlockSpec((None, None, fh, d), lambda ei, fi: (li, ei, fi, 0)),
        ],
        out_specs=[out_spec, out_spec],
        out_shape=[out, out],
        scratch_shapes=[
            pltpu.VMEM((d, 2 * fh), BF16),
            pltpu.VMEM((fh, d), BF16),
            pltpu.VMEM((batch * cap, d), F32),
        ],
        compiler_params=_cparams(("arbitrary", "arbitrary")),
        name="expert_ffn",
    )(xg, gc, w_gate_all, w_up_all, w_down_all)


def _combine_kernel(st_ref, nr_ref, h_ref, ps_ref, p_ref, yhi_hbm, ylo_hbm, wp_ref, wgt_ref,
                    gple_ref, gfin_ref, hout_ref, slab_ref, acc_ref, wpb_ref, wgb_ref, sem,
                    *, final, wp, cap, batch, tiles_per_batch):
    i = pl.program_id(0)
    t, e = ps_ref.shape
    slot = i & 1

    def window_copies(tile, q, buf):
        b = tile // tiles_per_batch
        copies = []
        for ei in range(e):
            ws = _window(st_ref[tile * e + ei], q, wp, cap)[1]
            row0 = pl.multiple_of((ei * batch + b) * cap + ws, BF16_ROWS)
            for part, src in enumerate((yhi_hbm, ylo_hbm)):
                copies.append(pltpu.make_async_copy(
                    src.at[pl.ds(row0, wp), :], slab_ref.at[buf, part, pl.ds(ei * wp, wp), :],
                    sem.at[buf]))
        return copies

    @pl.when(i == 0)
    def _():
        _cast_rows(wp_ref, wpb_ref)
        _cast_rows(wgt_ref, wgb_ref)
        for c in window_copies(0, 0, 0):
            c.start()

    @pl.when(i + 1 < pl.num_programs(0))
    def _():
        for c in window_copies(i + 1, 0, 1 - slot):
            c.start()

    ps = ps_ref[...]
    pos = ps >> 1
    sel = (ps & 1) > 0
    start = pos[0:1, :]
    k = e * wp
    wp_shift = wp.bit_length() - 1
    expand = _one_hot(lax.broadcasted_iota(I32, (e, k), 1) >> wp_shift
                      == lax.broadcasted_iota(I32, (e, k), 0))
    lane_w = (lax.broadcasted_iota(I32, (t, k), 1) & (wp - 1)).astype(F32)

    def apply_round(q):
        u, c = _window(start, q, wp, cap)
        prel = jnp.where(sel & (pos >= u) & (pos < u + wp), pos - c, -1)
        pexp = jnp.dot(prel.astype(F32).astype(BF16), expand, preferred_element_type=F32)
        oh = _one_hot(pexp == lane_w)
        acc_ref[...] += (jnp.dot(oh, slab_ref[slot, 0], preferred_element_type=F32)
                         + jnp.dot(oh, slab_ref[slot, 1], preferred_element_type=F32))

    acc_ref[...] = h_ref[...]
    for c in window_copies(i, 0, slot):
        c.wait()
    apply_round(0)

    def extra(q, _):
        for c in window_copies(i, q, slot):
            c.start()
        for c in window_copies(i, q, slot):
            c.wait()
        apply_round(q)
        return 0

    lax.fori_loop(1, nr_ref[i], extra, 0)

    hn = acc_ref[...]
    gate = _sigmoid(jnp.dot(_rms(hn, gple_ref[...]).astype(BF16), wgb_ref[...],
                            preferred_element_type=F32))
    hn = hn + gate * jnp.dot(p_ref[...].astype(BF16), wpb_ref[...], preferred_element_type=F32)
    if final:
        hn = _rms(hn, gfin_ref[...])
    hout_ref[...] = hn


def _combine(st_flat, nr, h, psel_tm, p_all, yhi, ylo, w_proj_all, w_gate_all, g_ple_all, li, g_final,
             batch, cap, tt, wp, final):
    n, d = h.shape
    e = psel_tm.shape[1]
    pd = p_all.shape[2]
    nt = n // tt
    grid_spec = pltpu.PrefetchScalarGridSpec(
        num_scalar_prefetch=2,
        grid=(nt,),
        in_specs=[
            pl.BlockSpec((tt, d), lambda i, st, nr: (i, 0)),
            pl.BlockSpec((tt, e), lambda i, st, nr: (i, 0)),
            pl.BlockSpec((None, tt, pd), lambda i, st, nr: (li, i, 0)),
            pl.BlockSpec(memory_space=pl.ANY),
            pl.BlockSpec(memory_space=pl.ANY),
            pl.BlockSpec((None, pd, d), lambda i, st, nr: (li, 0, 0)),
            pl.BlockSpec((None, d, d), lambda i, st, nr: (li, 0, 0)),
            pl.BlockSpec((None, 1, d), lambda i, st, nr: (li, 0, 0)),
            pl.BlockSpec((1, d), lambda i, st, nr: (0, 0)),
        ],
        out_specs=pl.BlockSpec((tt, d), lambda i, st, nr: (i, 0)),
        scratch_shapes=[
            pltpu.VMEM((2, 2, e * wp, d), BF16),
            pltpu.VMEM((tt, d), F32),
            pltpu.VMEM((pd, d), BF16),
            pltpu.VMEM((d, d), BF16),
            pltpu.SemaphoreType.DMA((2,)),
        ],
    )
    return pl.pallas_call(
        functools.partial(_combine_kernel, final=final, wp=wp, cap=cap, batch=batch,
                          tiles_per_batch=nt // batch),
        grid_spec=grid_spec,
        out_shape=jax.ShapeDtypeStruct((n, d), F32),
        compiler_params=_cparams(("arbitrary",)),
        name="combine_ple",
    )(st_flat, nr, h, psel_tm, p_all, yhi, ylo, w_proj_all, w_gate_all, g_ple_all,
      g_final.reshape(1, d))


RADIX = 16


def _fourier_in_kernel(h_ref, g_ref, w_ref, cs_ref, out_ref, wbf_ref, *, groups):
    @pl.when((pl.program_id(0) == 0) & (pl.program_id(1) == 0))
    def _():
        _cast_rows(w_ref, wbf_ref)

    xn = _rms(h_ref[...], g_ref[...])
    u = jnp.dot(xn.astype(BF16), wbf_ref[...], preferred_element_type=F32).astype(BF16)
    gd = u.shape[1] // groups
    for gi in range(groups):
        r = jnp.dot(u[:, gi * gd:(gi + 1) * gd], cs_ref[...], preferred_element_type=F32)
        out_ref[0, :, gi * gd:(gi + 1) * gd] = r[:, :gd].astype(BF16)
        out_ref[1, :, gi * gd:(gi + 1) * gd] = r[:, gd:].astype(BF16)


def _fourier_in(h, g_all, li, w_in_all, lj, cs_tab, batch, tm):
    n, d = h.shape
    s_len = n // batch
    nt = s_len // tm
    return pl.pallas_call(
        functools.partial(_fourier_in_kernel, groups=FOURIER_GROUPS),
        grid=(batch, nt),
        in_specs=[
            pl.BlockSpec((tm, d), lambda b, i: (b * nt + i, 0)),
            _layer_spec((1, d), li, 2),
            _layer_spec((d, d), lj, 2),
            pl.BlockSpec(cs_tab.shape, lambda b, i: (0, 0)),
        ],
        out_specs=pl.BlockSpec((None, 2, tm, d), lambda b, i: (b, 0, i, 0)),
        out_shape=jax.ShapeDtypeStruct((batch, 2, s_len, d), BF16),
        scratch_shapes=[pltpu.VMEM((d, d), BF16)],
        compiler_params=_cparams(("arbitrary", "arbitrary")),
        name="fourier_in",
    )(h, g_all, w_in_all, cs_tab)


def _seq_fft_kernel(y_ref, tp_ref, ta_ref, tb_ref, out_ref, ys_ref, scr_ref, *, scale):
    radix = ta_ref.shape[0]
    _, s_len, dc = y_ref.shape
    n_a = s_len // radix
    g = BF16_ROWS
    n_m = n_a // g
    rg = radix * g

    def regroup(m, _):
        base = pl.multiple_of(m * rg, rg)
        dst = pl.ds(pl.multiple_of(m * g, g), g)
        for part in range(2):
            yp = jnp.dot(tp_ref[...], y_ref[part, pl.ds(base, rg), :],
                         preferred_element_type=F32).astype(BF16)
            for n2 in range(radix):
                ys_ref[n2, part, dst, :] = yp[n2 * g:(n2 + 1) * g]
        return 0

    lax.fori_loop(0, n_m, regroup, 0, unroll=8)

    def stage_a(n2, _):
        y = ys_ref[n2].reshape(2 * n_a, dc)
        a = jnp.dot(ta_ref[n2], y, preferred_element_type=F32).astype(BF16)
        off = pl.multiple_of(n2 * g, g)
        for m in range(n_m):
            scr_ref[m, pl.ds(off, g), :] = a[m * g:(m + 1) * g]
            scr_ref[m, pl.ds(rg + off, g), :] = a[n_a + m * g:n_a + (m + 1) * g]
        return 0

    lax.fori_loop(0, radix, stage_a, 0, unroll=8)

    def stage_b(m, _):
        o = (jnp.dot(tb_ref[...], scr_ref[m], preferred_element_type=F32) * scale
             ).astype(out_ref.dtype)
        base = pl.multiple_of(m * g, g)
        for k2 in range(radix):
            out_ref[pl.ds(k2 * n_a + base, g), :] = o[k2 * g:(k2 + 1) * g]
        return 0

    lax.fori_loop(0, n_m, stage_b, 0, unroll=8)


def _seq_fft(ucs, tab_p, tab_a, tab_b, scale, dc):
    batch, _, s_len, d = ucs.shape
    radix = tab_a.shape[0]
    n_a = s_len // radix
    assert n_a % BF16_ROWS == 0
    rg = radix * BF16_ROWS
    return pl.pallas_call(
        functools.partial(_seq_fft_kernel, scale=scale),
        grid=(batch, d // dc),
        in_specs=[
            pl.BlockSpec((None, 2, s_len, dc), lambda b, j: (b, 0, 0, j)),
            pl.BlockSpec(tab_p.shape, lambda b, j: (0, 0)),
            pl.BlockSpec(tab_a.shape, lambda b, j: (0, 0, 0)),
            pl.BlockSpec(tab_b.shape, lambda b, j: (0, 0)),
        ],
        out_specs=pl.BlockSpec((None, s_len, dc), lambda b, j: (b, 0, j)),
        out_shape=jax.ShapeDtypeStruct((batch, s_len, d), BF16),
        scratch_shapes=[pltpu.VMEM((radix, 2, n_a, dc), BF16),
                        pltpu.VMEM((n_a // BF16_ROWS, 2 * rg, dc), BF16)],
        compiler_params=_cparams(("arbitrary", "arbitrary")),
        name="seq_fft",
    )(ucs, tab_p, tab_a, tab_b)


def _dft_tables(s_len, gd):
    n_a = s_len // RADIX
    i = np.arange(gd, dtype=np.int64)
    ang = ((i[:, None] * i[None, :]) % gd).astype(np.float64) * (2.0 * np.pi / gd)
    ch = np.concatenate([np.cos(ang), np.sin(ang)], axis=1)

    k1 = np.arange(n_a, dtype=np.int64)[None, :, None]
    n1 = np.arange(n_a, dtype=np.int64)[None, None, :]
    n2 = np.arange(RADIX, dtype=np.int64)[:, None, None]
    ang = ((k1 * (RADIX * n1 + n2)) % s_len).astype(np.float64) * (2.0 * np.pi / s_len)
    c, sn = np.cos(ang), np.sin(ang)
    tab_a = np.concatenate([np.concatenate([c, -sn], axis=2),
                            np.concatenate([-sn, -c], axis=2)], axis=1)

    r = np.arange(RADIX, dtype=np.int64)
    ang = ((r[:, None] * r[None, :]) % RADIX).astype(np.float64) * (2.0 * np.pi / RADIX)
    eye = np.eye(BF16_ROWS)
    tab_b = np.concatenate([np.kron(np.cos(ang), eye), np.kron(np.sin(ang), eye)], axis=1)

    rg = RADIX * BF16_ROWS
    src = (np.arange(rg) % BF16_ROWS) * RADIX + np.arange(rg) // BF16_ROWS
    tab_p = np.zeros((rg, rg))
    tab_p[np.arange(rg), src] = 1.0
    as_bf16 = lambda a: jnp.asarray(a, dtype=F32).astype(BF16)
    return as_bf16(ch), as_bf16(tab_p), as_bf16(tab_a), as_bf16(tab_b)


def _window_rows(tt, cap, s_len):
    wp = min(max(2 * tt * cap // s_len, 2 * BF16_ROWS), cap)
    assert wp & (wp - 1) == 0
    return wp


def _tile_tables(psel, cap, tt, wp):
    batch, e, s_len = psel.shape
    st = (psel >> 1)[:, :, ::tt]
    nxt = jnp.concatenate([st[:, :, 1:], jnp.full((batch, e, 1), cap, I32)], axis=2)
    span = nxt - (st & (-BF16_ROWS))
    rounds = jnp.max((span + (wp - 1)) // wp, axis=1).reshape(-1)
    return jnp.swapaxes(st, 1, 2).reshape(-1), jnp.maximum(rounds, 1)


def kernel(x, p, g_mix, g_ffn, g_ple, g_final, rg_w_in, rg_b_in, rg_conv_w, rg_conv_b,
           rg_gx_w, rg_gx_b, rg_ga_w, rg_ga_b, rg_lam, rg_w_out, rg_b_out,
           ft_w_in, ft_w_out, w_router, w_gate, w_up, w_down, ple_w_proj, ple_w_gate):
    batch, s_len, d = x.shape
    n = batch * s_len
    depth = g_mix.shape[0]
    n_exp = w_router.shape[2]
    cap = max(1, CAPACITY_FACTOR * s_len // n_exp)
    gd = d // FOURIER_GROUPS
    tm = min(512, s_len)
    chunk = min(512, s_len)
    tt_d = tt_c = min(256, s_len)
    wp_d, wp_c = _window_rows(tt_d, cap, s_len), _window_rows(tt_c, cap, s_len)

    h = x.reshape(n, d)
    ch_tab, fft_tab_p, fft_tab_a, fft_tab_b = _dft_tables(s_len, gd)
    dft_scale = 1.0 / math.sqrt(s_len * gd)
    row = lambda a: a.reshape(a.shape[0], 1, a.shape[-1])
    g_mix3, g_ffn3, g_ple3 = row(g_mix), row(g_ffn), row(g_ple)
    w_router_t = jnp.swapaxes(w_router, 1, 2)
    p3 = p.reshape(depth, n, p.shape[-1])
    rg_b_in3, rg_conv_b3, rg_b_out3 = row(rg_b_in), row(rg_conv_b), row(rg_b_out)
    ft_b_out3 = jnp.zeros((ft_w_out.shape[0], 1, d), F32)

    for i in range(depth):
        j = i // 2
        if i % 2 == 0:
            gg, u = _rg_in(h, g_mix3, i, rg_w_in, rg_b_in3, j, tm)
            gx, ga = rg_gx_w[j], rg_ga_w[j]
            wg_cat = jnp.concatenate([gx[0], ga[0], gx[1], ga[1]], axis=-1).astype(BF16)
            gxb, gab = rg_gx_b[j], rg_ga_b[j]
            bg_cat = jnp.concatenate([gxb[0], gab[0], gxb[1], gab[1]], axis=-1)[:, None, :]
            y = _rg_scan(u, gg, rg_conv_w, rg_conv_b3, wg_cat, bg_cat, rg_lam, j, batch, chunk)
            h, xn, lt = _mix_out(y, h, rg_w_out, j, rg_b_out3, g_ffn3, i, w_router_t, tm)
        else:
            ucs = _fourier_in(h, g_mix3, i, ft_w_in, j, ch_tab, batch, tm)
            f = _seq_fft(ucs, fft_tab_p, fft_tab_a, fft_tab_b, dft_scale, min(256, d))
            h, xn, lt = _mix_out(f.reshape(n, d), h, ft_w_out, j, ft_b_out3, g_ffn3, i,
                                 w_router_t, tm)

        psel, gc = _route(lt, batch, cap)
        st_d, nr_d = _tile_tables(psel, cap, tt_d, wp_d)
        st_c, nr_c = _tile_tables(psel, cap, tt_c, wp_c)
        psel_tm = jnp.swapaxes(psel, 1, 2).reshape(n, n_exp)
        xg = _dispatch(st_d, nr_d, xn, psel, batch, cap, tt_d, wp_d)
        yhi, ylo = _ffn(xg, gc, w_gate, w_up, w_down, i, 4)
        rows = n_exp * batch * cap
        h = _combine(st_c, nr_c, h, psel_tm, p3, yhi.reshape(rows, d), ylo.reshape(rows, d),
                     ple_w_proj, ple_w_gate, g_ple3, i, g_final, batch, cap, tt_c,
                     wp_c, final=(i == depth - 1))
    return h.reshape(batch, s_len, d)
```

```python
import functools
import math

import jax
import jax.numpy as jnp
import numpy as np
from jax import lax
from jax.experimental import pallas as pl
from jax.experimental.pallas import tpu as pltpu

EPS = 1e-6
RG_C = 8.0
FOURIER_GROUPS = 4
CAPACITY_FACTOR = 2

F32 = jnp.float32
BF16 = jnp.bfloat16
I32 = jnp.int32

SUBLANES = 8
BF16_ROWS = 16
GATE_LANES = 32
GATE_SHIFT = 5
VMEM_LIMIT = 56 * 1024 * 1024


def _cparams(sem):
    return pltpu.CompilerParams(dimension_semantics=sem, vmem_limit_bytes=VMEM_LIMIT)


def _layer_spec(tail, li, nargs):
    zeros = (0,) * len(tail)
    if nargs == 1:
        return pl.BlockSpec((None,) + tail, lambda i: (li,) + zeros)
    if nargs == 2:
        return pl.BlockSpec((None,) + tail, lambda i, j: (li,) + zeros)
    return pl.BlockSpec((None,) + tail, lambda i, j, k: (li,) + zeros)


def _rms(x, g):
    return x * lax.rsqrt(jnp.mean(x * x, axis=-1, keepdims=True) + EPS) * g


def _gelu_tanh(x):
    c = math.sqrt(2.0 / math.pi)
    return 0.5 * x * (1.0 + jnp.tanh(c * (x + 0.044715 * (x * x * x))))


def _sigmoid(x):
    return 1.0 / (1.0 + jnp.exp(-x))


def _one_hot(mask):
    return jnp.where(mask, 1.0, 0.0).astype(BF16)


def _cast_rows(src_ref, dst_ref, rows=256, unroll=False):
    n = src_ref.shape[0]
    rows = min(rows, n)
    assert n % rows == 0
    if unroll:
        for r in range(0, n, rows):
            dst_ref[pl.ds(r, rows), :] = src_ref[pl.ds(r, rows), :].astype(dst_ref.dtype)
        return

    def body(i, _):
        r = pl.multiple_of(i * rows, rows)
        dst_ref[pl.ds(r, rows), :] = src_ref[pl.ds(r, rows), :].astype(dst_ref.dtype)
        return 0

    lax.fori_loop(0, n // rows, body, 0)


def _rg_in_kernel(h_ref, g_ref, w_ref, b_ref, gg_ref, u_ref, wbf_ref):
    @pl.when(pl.program_id(0) == 0)
    def _():
        _cast_rows(w_ref, wbf_ref)

    xn = _rms(h_ref[...], g_ref[...])
    proj = jnp.dot(xn.astype(BF16), wbf_ref[...], preferred_element_type=F32) + b_ref[...]
    w = gg_ref.shape[-1]
    gg_ref[...] = _gelu_tanh(proj[:, :w])
    u_ref[...] = proj[:, w:]


def _rg_in(h, g_all, li, w_in_all, b_in_all, lj, tm):
    n, d = h.shape
    w2 = w_in_all.shape[2]
    w = w2 // 2
    return pl.pallas_call(
        _rg_in_kernel,
        grid=(n // tm,),
        in_specs=[
            pl.BlockSpec((tm, d), lambda i: (i, 0)),
            _layer_spec((1, d), li, 1),
            _layer_spec((d, w2), lj, 1),
            _layer_spec((1, w2), lj, 1),
        ],
        out_specs=[
            pl.BlockSpec((tm, w), lambda i: (i, 0)),
            pl.BlockSpec((tm, w), lambda i: (i, 0)),
        ],
        out_shape=[jax.ShapeDtypeStruct((n, w), F32), jax.ShapeDtypeStruct((n, w), F32)],
        scratch_shapes=[pltpu.VMEM((d, w2), BF16)],
        compiler_params=_cparams(("arbitrary",)),
        name="rg_in",
    )(h, g_all, w_in_all, b_in_all)


def _tile_scan(a, b, reverse):
    t, c = a.shape
    a = a.reshape(t // SUBLANES, SUBLANES, c)
    b = b.reshape(t // SUBLANES, SUBLANES, c)
    row = lax.broadcasted_iota(I32, a.shape, 1)
    s = 1
    while s < SUBLANES:
        if reverse:
            m = row < (SUBLANES - s)
            shift = SUBLANES - s
        else:
            m = row >= s
            shift = s
        a_sh = jnp.where(m, pltpu.roll(a, shift, 1), 1.0)
        b_sh = jnp.where(m, pltpu.roll(b, shift, 1), 0.0)
        b = a * b_sh + b
        a = a * a_sh
        s *= 2
    return a.reshape(t, c), b.reshape(t, c)


def _rg_scan_kernel(u_ref, gg_ref, cw_ref, cb_ref, wg_ref, bg_ref, lam_ref, o_ref,
                    af_ref, bf_ref, ab_ref, bb_ref, *, chunk):
    s_len, c = u_ref.shape
    t = chunk
    n_chunks = s_len // t
    halo = SUBLANES
    lw = t + 2 * halo

    lam = -lam_ref[...]
    sp = jnp.maximum(lam, 0.0) + jnp.log1p(jnp.exp(-jnp.abs(lam)))
    decay = (-RG_C) * sp
    cw = cw_ref[...]
    cb = cb_ref[...]
    bg = bg_ref[...]

    def pass1(ci, _):
        t0 = pl.multiple_of(ci * t, t)
        lo = pl.multiple_of(jnp.maximum(t0 - halo, 0), halo)
        hi = pl.multiple_of(jnp.minimum(t0 + t, s_len - halo), halo)
        prev = jnp.where(ci > 0, u_ref[pl.ds(lo, halo), :], 0.0)
        nxt = jnp.where(ci < n_chunks - 1, u_ref[pl.ds(hi, halo), :], 0.0)
        w = jnp.concatenate([prev, u_ref[pl.ds(t0, t), :], nxt], axis=0)
        conv = (cw[0:1] * pltpu.roll(w, 2, 0) + cw[1:2] * pltpu.roll(w, 1, 0)
                + cw[2:3] * w + cw[3:4] * pltpu.roll(w, lw - 1, 0))
        conv = conv[halo:halo + t] + cb
        gates = jnp.dot(conv.astype(BF16), wg_ref[...], preferred_element_type=F32) + bg
        for d, (a_ref, b_ref) in enumerate(((af_ref, bf_ref), (ab_ref, bb_ref))):
            i_t = _sigmoid(gates[:, (2 * d) * c:(2 * d + 1) * c])
            r_t = _sigmoid(gates[:, (2 * d + 1) * c:(2 * d + 2) * c])
            log_a = r_t * decay[d:d + 1]
            a_t = jnp.exp(log_a)
            b_t = jnp.sqrt(-jnp.tanh(log_a) * (a_t * a_t + 1.0)) * (i_t * conv)
            a_s, b_s = _tile_scan(a_t, b_t, reverse=(d == 1))
            a_ref[pl.ds(t0, t), :] = a_s
            b_ref[pl.ds(t0, t), :] = b_s
        return 0

    lax.fori_loop(0, n_chunks, pass1, 0)

    n_tiles = s_len // SUBLANES

    def pass2(k, carry):
        hf, hb = carry
        rf = pl.multiple_of(k * SUBLANES, SUBLANES)
        rb = pl.multiple_of((n_tiles - 1 - k) * SUBLANES, SUBLANES)
        h_f = bf_ref[pl.ds(rf, SUBLANES), :] + af_ref[pl.ds(rf, SUBLANES), :] * hf
        bf_ref[pl.ds(rf, SUBLANES), :] = h_f
        h_b = bb_ref[pl.ds(rb, SUBLANES), :] + ab_ref[pl.ds(rb, SUBLANES), :] * hb
        bb_ref[pl.ds(rb, SUBLANES), :] = h_b
        return h_f[SUBLANES - 1:SUBLANES, :], h_b[0:1, :]

    zero = jnp.zeros((1, c), F32)
    lax.fori_loop(0, n_tiles, pass2, (zero, zero), unroll=4)

    def pass3(ci, _):
        t0 = pl.multiple_of(ci * t, t)
        o_ref[pl.ds(t0, t), :] = ((bf_ref[pl.ds(t0, t), :] + bb_ref[pl.ds(t0, t), :])
                                  * gg_ref[pl.ds(t0, t), :]).astype(o_ref.dtype)
        return 0

    lax.fori_loop(0, n_chunks, pass3, 0)


def _rg_scan(u, gg, conv_w_all, conv_b_all, wg_cat, bg_cat, lam_all, lj, batch, chunk):
    n, w = u.shape
    s_len = n // batch
    heads, c, c4 = wg_cat.shape
    kw = conv_w_all.shape[1]
    u3 = u.reshape(batch, s_len, w)
    gg3 = gg.reshape(batch, s_len, w)
    out = pl.pallas_call(
        functools.partial(_rg_scan_kernel, chunk=chunk),
        grid=(batch, heads),
        in_specs=[
            pl.BlockSpec((None, s_len, c), lambda b, h: (b, 0, h)),
            pl.BlockSpec((None, s_len, c), lambda b, h: (b, 0, h)),
            pl.BlockSpec((None, kw, c), lambda b, h: (lj, 0, h)),
            pl.BlockSpec((None, 1, c), lambda b, h: (lj, 0, h)),
            pl.BlockSpec((None, c, c4), lambda b, h: (h, 0, 0)),
            pl.BlockSpec((None, 1, c4), lambda b, h: (h, 0, 0)),
            pl.BlockSpec((None, 2, c), lambda b, h: (lj, 0, h)),
        ],
        out_specs=pl.BlockSpec((None, s_len, c), lambda b, h: (b, 0, h)),
        out_shape=jax.ShapeDtypeStruct((batch, s_len, w), BF16),
        scratch_shapes=[pltpu.VMEM((s_len, c), F32)] * 4,
        compiler_params=_cparams(("arbitrary", "arbitrary")),
        name="rg_scan",
    )(u3, gg3, conv_w_all, conv_b_all, wg_cat, bg_cat, lam_all)
    return out.reshape(n, w)


def _mix_out_kernel(y_ref, h_ref, w_ref, b_ref, g_ref, wr_ref, hout_ref, xn_ref, lt_ref, wbf_ref):
    @pl.when(pl.program_id(0) == 0)
    def _():
        _cast_rows(w_ref, wbf_ref)

    hn = (h_ref[...] + jnp.dot(y_ref[...].astype(BF16), wbf_ref[...], preferred_element_type=F32)
          + b_ref[...])
    hout_ref[...] = hn
    xn = _rms(hn, g_ref[...])
    xn_ref[...] = xn.astype(BF16)
    lt_ref[...] = lax.dot_general(wr_ref[...], xn, (((1,), (1,)), ((), ())),
                                  precision=lax.Precision.HIGHEST, preferred_element_type=F32)


def _mix_out(y, h, w_all, lj, b_all, g_all, li, w_router_t_all, tm):
    n, d = h.shape
    k = y.shape[1]
    e = w_router_t_all.shape[1]
    return pl.pallas_call(
        _mix_out_kernel,
        grid=(n // tm,),
        in_specs=[
            pl.BlockSpec((tm, k), lambda i: (i, 0)),
            pl.BlockSpec((tm, d), lambda i: (i, 0)),
            _layer_spec((k, d), lj, 1),
            _layer_spec((1, d), lj, 1),
            _layer_spec((1, d), li, 1),
            _layer_spec((e, d), li, 1),
        ],
        out_specs=[
            pl.BlockSpec((tm, d), lambda i: (i, 0)),
            pl.BlockSpec((tm, d), lambda i: (i, 0)),
            pl.BlockSpec((e, tm), lambda i: (0, i)),
        ],
        out_shape=[jax.ShapeDtypeStruct((n, d), F32), jax.ShapeDtypeStruct((n, d), BF16),
                   jax.ShapeDtypeStruct((e, n), F32)],
        scratch_shapes=[pltpu.VMEM((k, d), BF16)],
        compiler_params=_cparams(("arbitrary",)),
        name="mix_out",
    )(y, h, w_all, b_all, g_all, w_router_t_all)


def _excl_cumsum_lanes(x):
    s_len = x.shape[1]
    lane = lax.broadcasted_iota(I32, x.shape, 1)
    acc = x
    s = 1
    while s < s_len:
        acc = acc + jnp.where(lane >= s, pltpu.roll(acc, s, 1), 0)
        s *= 2
    return acc - x


def _route_kernel(lt_ref, ps_ref, gc_ref, *, cap):
    e, s_len = lt_ref.shape
    lg = lt_ref[...]
    ex = jnp.exp(lg - jnp.max(lg, axis=0, keepdims=True))
    aff = ex / jnp.sum(ex, axis=0, keepdims=True)

    def count(mask):
        return jnp.sum(mask.astype(F32), axis=1, keepdims=True).astype(I32)

    def radix(i, prefix):
        cand = prefix | lax.shift_left(jnp.int32(1), 30 - i)
        cand_f = lax.bitcast_convert_type(cand, F32)
        return jnp.where(count(aff >= cand_f) >= cap, cand, prefix)

    thr_bits = lax.fori_loop(0, 31, radix, jnp.zeros((e, 1), I32))
    thr = lax.bitcast_convert_type(thr_bits, F32)
    above = lax.bitcast_convert_type(thr_bits + 1, F32)
    gt = aff >= above
    eq = (aff >= thr) & jnp.logical_not(gt)
    need = cap - count(gt)
    eq_rank = _excl_cumsum_lanes(eq.astype(I32))
    sel_i = (gt | (eq & (eq_rank < need))).astype(I32)
    pos = _excl_cumsum_lanes(sel_i)
    ps_ref[...] = pos * 2 + sel_i

    rh = max(cap // GATE_LANES, BF16_ROWS)
    row_hi = lax.broadcasted_iota(I32, (rh, s_len), 0)
    row_lo = lax.broadcasted_iota(I32, (GATE_LANES, s_len), 0)
    a1 = aff.astype(BF16).astype(F32)
    a2 = (aff - a1).astype(BF16).astype(F32)
    a3 = (aff - a1 - a2).astype(BF16).astype(F32)
    slot_r = lax.broadcasted_iota(I32, (cap, rh), 0)
    pick_hi = _one_hot((slot_r >> GATE_SHIFT) == lax.broadcasted_iota(I32, (cap, rh), 1))
    pick_lo = ((lax.broadcasted_iota(I32, (cap, GATE_LANES), 0) & (GATE_LANES - 1))
               == lax.broadcasted_iota(I32, (cap, GATE_LANES), 1))
    for ei in range(e):
        pe = pos[ei:ei + 1, :]
        oh_hi = ((pe >> GATE_SHIFT) == row_hi) & (sel_i[ei:ei + 1, :] > 0)
        oh_lo = _one_hot((pe & (GATE_LANES - 1)) == row_lo)
        lhs = jnp.concatenate([jnp.where(oh_hi, a[ei:ei + 1, :], 0.0) for a in (a1, a2, a3)],
                              axis=0).astype(BF16)
        r = lax.dot_general(lhs, oh_lo, (((1,), (1,)), ((), ())), preferred_element_type=F32)
        r = r.astype(BF16)
        rows = sum(jnp.dot(pick_hi, r[k * rh:(k + 1) * rh], preferred_element_type=F32)
                   for k in range(3))
        gc_ref[ei] = jnp.sum(jnp.where(pick_lo, rows, 0.0), axis=1, keepdims=True)


def _route(lt, batch, cap):
    e, n = lt.shape
    s_len = n // batch
    assert cap % GATE_LANES == 0
    return pl.pallas_call(
        functools.partial(_route_kernel, cap=cap),
        grid=(batch,),
        in_specs=[pl.BlockSpec((e, s_len), lambda b: (0, b))],
        out_specs=[
            pl.BlockSpec((None, e, s_len), lambda b: (b, 0, 0)),
            pl.BlockSpec((None, e, cap, 1), lambda b: (b, 0, 0, 0)),
        ],
        out_shape=[
            jax.ShapeDtypeStruct((batch, e, s_len), I32),
            jax.ShapeDtypeStruct((batch, e, cap, 1), F32),
        ],
        compiler_params=_cparams(("arbitrary",)),
        name="route",
    )(lt)


def _window(start, q, wp, cap):
    u = (start & (-BF16_ROWS)) + q * wp
    return u, jnp.minimum(u, cap - wp)


def _dispatch_kernel(st_ref, nr_ref, xn_ref, ps_ref, out_ref, *, wp):
    b = pl.program_id(0)
    i = pl.program_id(1)
    tile = b * pl.num_programs(1) + i
    e, cap, d = out_ref.shape
    t = xn_ref.shape[0]

    @pl.when(i == 0)
    def _():
        def zero(ei, _):
            out_ref[ei] = jnp.zeros((cap, d), BF16)
            return 0

        lax.fori_loop(0, e, zero, 0)

    x = xn_ref[...]
    ps = ps_ref[...]
    pos = ps >> 1
    sel = (ps & 1) > 0
    start = pos[:, 0:1]
    sub = lax.broadcasted_iota(I32, (wp, t), 0)

    def one_round(q):
        u, c = _window(start, q, wp, cap)
        prel = jnp.where(sel & (pos >= u) & (pos < u + wp), pos - c, -1)
        oh = jnp.concatenate([_one_hot(sub == prel[ei:ei + 1, :]) for ei in range(e)], axis=0)
        win = jnp.dot(oh, x, preferred_element_type=F32)
        for ei in range(e):
            ws = _window(st_ref[tile * e + ei], q, wp, cap)[1]
            rows = pl.ds(pl.multiple_of(ws, BF16_ROWS), wp)
            out_ref[ei, rows, :] = out_ref[ei, rows, :] + win[ei * wp:(ei + 1) * wp].astype(BF16)

    one_round(0)

    def extra(q, _):
        one_round(q)
        return 0

    lax.fori_loop(1, nr_ref[tile], extra, 0)


def _dispatch(st_flat, nr, xn, psel, batch, cap, tt, wp):
    n, d = xn.shape
    s_len = n // batch
    e = psel.shape[1]
    nt = s_len // tt
    grid_spec = pltpu.PrefetchScalarGridSpec(
        num_scalar_prefetch=2,
        grid=(batch, nt),
        in_specs=[
            pl.BlockSpec((tt, d), lambda b, i, st, nr: (b * nt + i, 0)),
            pl.BlockSpec((None, e, tt), lambda b, i, st, nr: (b, 0, i)),
        ],
        out_specs=pl.BlockSpec((e, None, cap, d), lambda b, i, st, nr: (0, b, 0, 0)),
    )
    return pl.pallas_call(
        functools.partial(_dispatch_kernel, wp=wp),
        grid_spec=grid_spec,
        out_shape=jax.ShapeDtypeStruct((e, batch, cap, d), BF16),
        compiler_params=_cparams(("arbitrary", "arbitrary")),
        name="dispatch",
    )(st_flat, nr, xn, psel)


def _ffn_kernel(xg_ref, gc_ref, wg_ref, wu_ref, wd_ref, out_ref, wgub_ref, wdb_ref, acc_ref):
    fh = pl.program_id(1)
    batch, cap, _ = xg_ref.shape

    fw = wg_ref.shape[1]
    _cast_rows(wg_ref, wgub_ref.at[:, pl.ds(0, fw)], unroll=True)
    _cast_rows(wu_ref, wgub_ref.at[:, pl.ds(fw, fw)], unroll=True)
    _cast_rows(wd_ref, wdb_ref, unroll=True)

    nb = 2 if batch % 2 == 0 else 1
    for b in range(0, batch, nb):
        x = xg_ref[pl.ds(b, nb)].reshape(nb * cap, xg_ref.shape[2])
        hgu = jnp.dot(x, wgub_ref[...], preferred_element_type=F32)
        hgate, hup = hgu[:, :fw], hgu[:, fw:]
        hg = (hgate * _sigmoid(hgate) * hup).astype(BF16)
        y = jnp.dot(hg, wdb_ref[...], preferred_element_type=F32)
        rows = pl.ds(b * cap, nb * cap)

        @pl.when(fh == 0)
        def _():
            acc_ref[rows, :] = y

        @pl.when(fh > 0)
        def _():
            acc_ref[rows, :] += y

    @pl.when(fh == pl.num_programs(1) - 1)
    def _():
        def scale(i, _):
            rows = pl.ds(pl.multiple_of(i * cap, cap), cap)
            out_ref[rows, :] = (acc_ref[rows, :] * gc_ref[i]).astype(BF16)
            return 0

        lax.fori_loop(0, batch, scale, 0)


def _ffn(xg, gc, w_gate_all, w_up_all, w_down_all, li, n_fh):
    e, batch, cap, d = xg.shape
    f = w_gate_all.shape[3]
    fh = f // n_fh
    return pl.pallas_call(
        _ffn_kernel,
        grid=(e, n_fh),
        in_specs=[
            pl.BlockSpec((None, batch, cap, d), lambda ei, fi: (ei, 0, 0, 0)),
            pl.BlockSpec((batch, None, cap, 1), lambda ei, fi: (0, ei, 0, 0)),
            pl.BlockSpec((None, None, d, fh), lambda ei, fi: (li, ei, 0, fi)),
            pl.BlockSpec((None, None, d, fh), lambda ei, fi: (li, ei, 0, fi)),
            pl.BlockSpec((None, None, fh, d), lambda ei, fi: (li, ei, fi, 0)),
        ],
        out_specs=pl.BlockSpec((None, batch * cap, d), lambda ei, fi: (ei, 0, 0)),
        out_shape=jax.ShapeDtypeStruct((e, batch * cap, d), BF16),
        scratch_shapes=[
            pltpu.VMEM((d, 2 * fh), BF16),
            pltpu.VMEM((fh, d), BF16),
            pltpu.VMEM((batch * cap, d), F32),
        ],
        compiler_params=_cparams(("arbitrary", "arbitrary")),
        name="expert_ffn",
    )(xg, gc, w_gate_all, w_up_all, w_down_all)


def _combine_kernel(st_ref, nr_ref, h_ref, ps_ref, p_ref, yg_hbm, wp_ref, wgt_ref,
                    gple_ref, gfin_ref, hout_ref, slab_ref, acc_ref, wpb_ref, wgb_ref, sem,
                    *, final, wp, cap, batch, tiles_per_batch):
    i = pl.program_id(0)
    t, e = ps_ref.shape
    slot = i & 1

    def window_copies(tile, q, buf):
        b = tile // tiles_per_batch
        copies = []
        for ei in range(e):
            ws = _window(st_ref[tile * e + ei], q, wp, cap)[1]
            row0 = pl.multiple_of((ei * batch + b) * cap + ws, BF16_ROWS)
            copies.append(pltpu.make_async_copy(
                yg_hbm.at[pl.ds(row0, wp), :], slab_ref.at[buf, pl.ds(ei * wp, wp), :],
                sem.at[buf]))
        return copies

    @pl.when(i == 0)
    def _():
        _cast_rows(wp_ref, wpb_ref)
        _cast_rows(wgt_ref, wgb_ref)
        for c in window_copies(0, 0, 0):
            c.start()

    @pl.when(i + 1 < pl.num_programs(0))
    def _():
        for c in window_copies(i + 1, 0, 1 - slot):
            c.start()

    ps = ps_ref[...]
    pos = ps >> 1
    sel = (ps & 1) > 0
    start = pos[0:1, :]
    k = e * wp
    wp_shift = wp.bit_length() - 1
    expand = _one_hot(lax.broadcasted_iota(I32, (e, k), 1) >> wp_shift
                      == lax.broadcasted_iota(I32, (e, k), 0))
    lane_w = (lax.broadcasted_iota(I32, (t, k), 1) & (wp - 1)).astype(F32)

    def apply_round(q):
        u, c = _window(start, q, wp, cap)
        prel = jnp.where(sel & (pos >= u) & (pos < u + wp), pos - c, -1)
        pexp = jnp.dot(prel.astype(F32).astype(BF16), expand, preferred_element_type=F32)
        oh = _one_hot(pexp == lane_w)
        acc_ref[...] += jnp.dot(oh, slab_ref[slot], preferred_element_type=F32)

    acc_ref[...] = h_ref[...]
    for c in window_copies(i, 0, slot):
        c.wait()
    apply_round(0)

    def extra(q, _):
        for c in window_copies(i, q, slot):
            c.start()
        for c in window_copies(i, q, slot):
            c.wait()
        apply_round(q)
        return 0

    lax.fori_loop(1, nr_ref[i], extra, 0)

    hn = acc_ref[...]
    gate = _sigmoid(jnp.dot(_rms(hn, gple_ref[...]).astype(BF16), wgb_ref[...],
                            preferred_element_type=F32))
    hn = hn + gate * jnp.dot(p_ref[...].astype(BF16), wpb_ref[...], preferred_element_type=F32)
    if final:
        hn = _rms(hn, gfin_ref[...])
    hout_ref[...] = hn


def _combine(st_flat, nr, h, psel_tm, p_all, yg, w_proj_all, w_gate_all, g_ple_all, li, g_final,
             batch, cap, tt, wp, final):
    n, d = h.shape
    e = psel_tm.shape[1]
    pd = p_all.shape[2]
    nt = n // tt
    grid_spec = pltpu.PrefetchScalarGridSpec(
        num_scalar_prefetch=2,
        grid=(nt,),
        in_specs=[
            pl.BlockSpec((tt, d), lambda i, st, nr: (i, 0)),
            pl.BlockSpec((tt, e), lambda i, st, nr: (i, 0)),
            pl.BlockSpec((None, tt, pd), lambda i, st, nr: (li, i, 0)),
            pl.BlockSpec(memory_space=pl.ANY),
            pl.BlockSpec((None, pd, d), lambda i, st, nr: (li, 0, 0)),
            pl.BlockSpec((None, d, d), lambda i, st, nr: (li, 0, 0)),
            pl.BlockSpec((None, 1, d), lambda i, st, nr: (li, 0, 0)),
            pl.BlockSpec((1, d), lambda i, st, nr: (0, 0)),
        ],
        out_specs=pl.BlockSpec((tt, d), lambda i, st, nr: (i, 0)),
        scratch_shapes=[
            pltpu.VMEM((2, e * wp, d), BF16),
            pltpu.VMEM((tt, d), F32),
            pltpu.VMEM((pd, d), BF16),
            pltpu.VMEM((d, d), BF16),
            pltpu.SemaphoreType.DMA((2,)),
        ],
    )
    return pl.pallas_call(
        functools.partial(_combine_kernel, final=final, wp=wp, cap=cap, batch=batch,
                          tiles_per_batch=nt // batch),
        grid_spec=grid_spec,
        out_shape=jax.ShapeDtypeStruct((n, d), F32),
        compiler_params=_cparams(("arbitrary",)),
        name="combine_ple",
    )(st_flat, nr, h, psel_tm, p_all, yg, w_proj_all, w_gate_all, g_ple_all,
      g_final.reshape(1, d))


RADIX = 16


def _fourier_in_kernel(h_ref, g_ref, w_ref, cs_ref, out_ref, wbf_ref, *, groups):
    @pl.when((pl.program_id(0) == 0) & (pl.program_id(1) == 0))
    def _():
        _cast_rows(w_ref, wbf_ref)

    xn = _rms(h_ref[...], g_ref[...])
    u = jnp.dot(xn.astype(BF16), wbf_ref[...], preferred_element_type=F32).astype(BF16)
    gd = u.shape[1] // groups
    for gi in range(groups):
        r = jnp.dot(u[:, gi * gd:(gi + 1) * gd], cs_ref[...], preferred_element_type=F32)
        out_ref[0, :, gi * gd:(gi + 1) * gd] = r[:, :gd].astype(BF16)
        out_ref[1, :, gi * gd:(gi + 1) * gd] = r[:, gd:].astype(BF16)


def _fourier_in(h, g_all, li, w_in_all, lj, cs_tab, batch, tm):
    n, d = h.shape
    s_len = n // batch
    nt = s_len // tm
    return pl.pallas_call(
        functools.partial(_fourier_in_kernel, groups=FOURIER_GROUPS),
        grid=(batch, nt),
        in_specs=[
            pl.BlockSpec((tm, d), lambda b, i: (b * nt + i, 0)),
            _layer_spec((1, d), li, 2),
            _layer_spec((d, d), lj, 2),
            pl.BlockSpec(cs_tab.shape, lambda b, i: (0, 0)),
        ],
        out_specs=pl.BlockSpec((None, 2, tm, d), lambda b, i: (b, 0, i, 0)),
        out_shape=jax.ShapeDtypeStruct((batch, 2, s_len, d), BF16),
        scratch_shapes=[pltpu.VMEM((d, d), BF16)],
        compiler_params=_cparams(("arbitrary", "arbitrary")),
        name="fourier_in",
    )(h, g_all, w_in_all, cs_tab)


def _seq_fft_kernel(y_ref, tp_ref, ta_ref, tb_ref, out_ref, ys_ref, scr_ref, *, scale):
    radix = ta_ref.shape[0]
    _, s_len, dc = y_ref.shape
    n_a = s_len // radix
    g = BF16_ROWS
    n_m = n_a // g
    rg = radix * g

    def regroup(m, _):
        base = pl.multiple_of(m * rg, rg)
        dst = pl.ds(pl.multiple_of(m * g, g), g)
        for part in range(2):
            yp = jnp.dot(tp_ref[...], y_ref[part, pl.ds(base, rg), :],
                         preferred_element_type=F32).astype(BF16)
            for n2 in range(radix):
                ys_ref[n2, part, dst, :] = yp[n2 * g:(n2 + 1) * g]
        return 0

    lax.fori_loop(0, n_m, regroup, 0, unroll=8)

    def stage_a(n2, _):
        y = ys_ref[n2].reshape(2 * n_a, dc)
        a = jnp.dot(ta_ref[n2], y, preferred_element_type=F32).astype(BF16)
        off = pl.multiple_of(n2 * g, g)
        for m in range(n_m):
            scr_ref[m, pl.ds(off, g), :] = a[m * g:(m + 1) * g]
            scr_ref[m, pl.ds(rg + off, g), :] = a[n_a + m * g:n_a + (m + 1) * g]
        return 0

    lax.fori_loop(0, radix, stage_a, 0, unroll=8)

    def stage_b(m, _):
        o = (jnp.dot(tb_ref[...], scr_ref[m], preferred_element_type=F32) * scale
             ).astype(out_ref.dtype)
        base = pl.multiple_of(m * g, g)
        for k2 in range(radix):
            out_ref[pl.ds(k2 * n_a + base, g), :] = o[k2 * g:(k2 + 1) * g]
        return 0

    lax.fori_loop(0, n_m, stage_b, 0, unroll=8)


def _seq_fft(ucs, tab_p, tab_a, tab_b, scale, dc):
    batch, _, s_len, d = ucs.shape
    radix = tab_a.shape[0]
    n_a = s_len // radix
    assert n_a % BF16_ROWS == 0
    rg = radix * BF16_ROWS
    return pl.pallas_call(
        functools.partial(_seq_fft_kernel, scale=scale),
        grid=(batch, d // dc),
        in_specs=[
            pl.BlockSpec((None, 2, s_len, dc), lambda b, j: (b, 0, 0, j)),
            pl.BlockSpec(tab_p.shape, lambda b, j: (0, 0)),
            pl.BlockSpec(tab_a.shape, lambda b, j: (0, 0, 0)),
            pl.BlockSpec(tab_b.shape, lambda b, j: (0, 0)),
        ],
        out_specs=pl.BlockSpec((None, s_len, dc), lambda b, j: (b, 0, j)),
        out_shape=jax.ShapeDtypeStruct((batch, s_len, d), BF16),
        scratch_shapes=[pltpu.VMEM((radix, 2, n_a, dc), BF16),
                        pltpu.VMEM((n_a // BF16_ROWS, 2 * rg, dc), BF16)],
        compiler_params=_cparams(("arbitrary", "arbitrary")),
        name="seq_fft",
    )(ucs, tab_p, tab_a, tab_b)


def _dft_tables(s_len, gd):
    n_a = s_len // RADIX
    i = np.arange(gd, dtype=np.int64)
    ang = ((i[:, None] * i[None, :]) % gd).astype(np.float64) * (2.0 * np.pi / gd)
    ch = np.concatenate([np.cos(ang), np.sin(ang)], axis=1)

    k1 = np.arange(n_a, dtype=np.int64)[None, :, None]
    n1 = np.arange(n_a, dtype=np.int64)[None, None, :]
    n2 = np.arange(RADIX, dtype=np.int64)[:, None, None]
    ang = ((k1 * (RADIX * n1 + n2)) % s_len).astype(np.float64) * (2.0 * np.pi / s_len)
    c, sn = np.cos(ang), np.sin(ang)
    tab_a = np.concatenate([np.concatenate([c, -sn], axis=2),
                            np.concatenate([-sn, -c], axis=2)], axis=1)

    r = np.arange(RADIX, dtype=np.int64)
    ang = ((r[:, None] * r[None, :]) % RADIX).astype(np.float64) * (2.0 * np.pi / RADIX)
    eye = np.eye(BF16_ROWS)
    tab_b = np.concatenate([np.kron(np.cos(ang), eye), np.kron(np.sin(ang), eye)], axis=1)

    rg = RADIX * BF16_ROWS
    src = (np.arange(rg) % BF16_ROWS) * RADIX + np.arange(rg) // BF16_ROWS
    tab_p = np.zeros((rg, rg))
    tab_p[np.arange(rg), src] = 1.0
    as_bf16 = lambda a: jnp.asarray(a, dtype=F32).astype(BF16)
    return as_bf16(ch), as_bf16(tab_p), as_bf16(tab_a), as_bf16(tab_b)


def _window_rows(tt, cap, s_len):
    wp = min(max(2 * tt * cap // s_len, 2 * BF16_ROWS), cap)
    assert wp & (wp - 1) == 0
    return wp


def _tile_tables(psel, cap, tt, wp):
    batch, e, s_len = psel.shape
    st = (psel >> 1)[:, :, ::tt]
    nxt = jnp.concatenate([st[:, :, 1:], jnp.full((batch, e, 1), cap, I32)], axis=2)
    span = nxt - (st & (-BF16_ROWS))
    rounds = jnp.max((span + (wp - 1)) // wp, axis=1).reshape(-1)
    return jnp.swapaxes(st, 1, 2).reshape(-1), jnp.maximum(rounds, 1)


def kernel(x, p, g_mix, g_ffn, g_ple, g_final, rg_w_in, rg_b_in, rg_conv_w, rg_conv_b,
           rg_gx_w, rg_gx_b, rg_ga_w, rg_ga_b, rg_lam, rg_w_out, rg_b_out,
           ft_w_in, ft_w_out, w_router, w_gate, w_up, w_down, ple_w_proj, ple_w_gate):
    batch, s_len, d = x.shape
    n = batch * s_len
    depth = g_mix.shape[0]
    n_exp = w_router.shape[2]
    cap = max(1, CAPACITY_FACTOR * s_len // n_exp)
    gd = d // FOURIER_GROUPS
    tm = min(512, s_len)
    chunk = min(512, s_len)
    tt_d = tt_c = min(256, s_len)
    wp_d, wp_c = _window_rows(tt_d, cap, s_len), _window_rows(tt_c, cap, s_len)

    h = x.reshape(n, d)
    ch_tab, fft_tab_p, fft_tab_a, fft_tab_b = _dft_tables(s_len, gd)
    dft_scale = 1.0 / math.sqrt(s_len * gd)
    row = lambda a: a.reshape(a.shape[0], 1, a.shape[-1])
    g_mix3, g_ffn3, g_ple3 = row(g_mix), row(g_ffn), row(g_ple)
    w_router_t = jnp.swapaxes(w_router, 1, 2)
    p3 = p.reshape(depth, n, p.shape[-1])
    rg_b_in3, rg_conv_b3, rg_b_out3 = row(rg_b_in), row(rg_conv_b), row(rg_b_out)
    ft_b_out3 = jnp.zeros((ft_w_out.shape[0], 1, d), F32)

    for i in range(depth):
        j = i // 2
        if i % 2 == 0:
            gg, u = _rg_in(h, g_mix3, i, rg_w_in, rg_b_in3, j, tm)
            gx, ga = rg_gx_w[j], rg_ga_w[j]
            wg_cat = jnp.concatenate([gx[0], ga[0], gx[1], ga[1]], axis=-1).astype(BF16)
            gxb, gab = rg_gx_b[j], rg_ga_b[j]
            bg_cat = jnp.concatenate([gxb[0], gab[0], gxb[1], gab[1]], axis=-1)[:, None, :]
            y = _rg_scan(u, gg, rg_conv_w, rg_conv_b3, wg_cat, bg_cat, rg_lam, j, batch, chunk)
            h, xn, lt = _mix_out(y, h, rg_w_out, j, rg_b_out3, g_ffn3, i, w_router_t, tm)
        else:
            ucs = _fourier_in(h, g_mix3, i, ft_w_in, j, ch_tab, batch, tm)
            f = _seq_fft(ucs, fft_tab_p, fft_tab_a, fft_tab_b, dft_scale, min(256, d))
            h, xn, lt = _mix_out(f.reshape(n, d), h, ft_w_out, j, ft_b_out3, g_ffn3, i,
                                 w_router_t, tm)

        psel, gc = _route(lt, batch, cap)
        st_d, nr_d = _tile_tables(psel, cap, tt_d, wp_d)
        st_c, nr_c = _tile_tables(psel, cap, tt_c, wp_c)
        psel_tm = jnp.swapaxes(psel, 1, 2).reshape(n, n_exp)
        xg = _dispatch(st_d, nr_d, xn, psel, batch, cap, tt_d, wp_d)
        yg = _ffn(xg, gc, w_gate, w_up, w_down, i, 4)
        h = _combine(st_c, nr_c, h, psel_tm, p3, yg.reshape(n_exp * batch * cap, d),
                     ple_w_proj, ple_w_gate, g_ple3, i, g_final, batch, cap, tt_c,
                     wp_c, final=(i == depth - 1))
    return h.reshape(batch, s_len, d)
```

```python
import functools
import math

import jax
import jax.numpy as jnp
import numpy as np
from jax import lax
from jax.experimental import pallas as pl
from jax.experimental.pallas import tpu as pltpu

EPS = 1e-6
RG_C = 8.0
FOURIER_GROUPS = 4
CAPACITY_FACTOR = 2

F32 = jnp.float32
BF16 = jnp.bfloat16
I32 = jnp.int32

SUBLANES = 8
BF16_ROWS = 16
GATE_LANES = 32
GATE_SHIFT = 5
VMEM_LIMIT = 56 * 1024 * 1024


def _cparams(sem):
    return pltpu.CompilerParams(dimension_semantics=sem, vmem_limit_bytes=VMEM_LIMIT)


def _layer_spec(tail, li, nargs):
    zeros = (0,) * len(tail)
    if nargs == 1:
        return pl.BlockSpec((None,) + tail, lambda i: (li,) + zeros)
    if nargs == 2:
        return pl.BlockSpec((None,) + tail, lambda i, j: (li,) + zeros)
    return pl.BlockSpec((None,) + tail, lambda i, j, k: (li,) + zeros)


def _rms(x, g):
    return x * lax.rsqrt(jnp.mean(x * x, axis=-1, keepdims=True) + EPS) * g


def _gelu_tanh(x):
    c = math.sqrt(2.0 / math.pi)
    return 0.5 * x * (1.0 + jnp.tanh(c * (x + 0.044715 * (x * x * x))))


def _sigmoid(x):
    return 1.0 / (1.0 + jnp.exp(-x))


def _one_hot(mask):
    return jnp.where(mask, 1.0, 0.0).astype(BF16)


def _cast_rows(src_ref, dst_ref, rows=256, unroll=False):
    n = src_ref.shape[0]
    rows = min(rows, n)
    assert n % rows == 0
    if unroll:
        for r in range(0, n, rows):
            dst_ref[pl.ds(r, rows), :] = src_ref[pl.ds(r, rows), :].astype(dst_ref.dtype)
        return

    def body(i, _):
        r = pl.multiple_of(i * rows, rows)
        dst_ref[pl.ds(r, rows), :] = src_ref[pl.ds(r, rows), :].astype(dst_ref.dtype)
        return 0

    lax.fori_loop(0, n // rows, body, 0)


def _rg_in_kernel(h_ref, g_ref, w_ref, b_ref, gg_ref, u_ref, wbf_ref):
    @pl.when(pl.program_id(0) == 0)
    def _():
        _cast_rows(w_ref, wbf_ref)

    xn = _rms(h_ref[...], g_ref[...])
    proj = jnp.dot(xn.astype(BF16), wbf_ref[...], preferred_element_type=F32) + b_ref[...]
    w = gg_ref.shape[-1]
    gg_ref[...] = _gelu_tanh(proj[:, :w])
    u_ref[...] = proj[:, w:]


def _rg_in(h, g_all, li, w_in_all, b_in_all, lj, tm):
    n, d = h.shape
    w2 = w_in_all.shape[2]
    w = w2 // 2
    return pl.pallas_call(
        _rg_in_kernel,
        grid=(n // tm,),
        in_specs=[
            pl.BlockSpec((tm, d), lambda i: (i, 0)),
            _layer_spec((1, d), li, 1),
            _layer_spec((d, w2), lj, 1),
            _layer_spec((1, w2), lj, 1),
        ],
        out_specs=[
            pl.BlockSpec((tm, w), lambda i: (i, 0)),
            pl.BlockSpec((tm, w), lambda i: (i, 0)),
        ],
        out_shape=[jax.ShapeDtypeStruct((n, w), F32), jax.ShapeDtypeStruct((n, w), F32)],
        scratch_shapes=[pltpu.VMEM((d, w2), BF16)],
        compiler_params=_cparams(("arbitrary",)),
        name="rg_in",
    )(h, g_all, w_in_all, b_in_all)


def _tile_scan(a, b, reverse):
    t, c = a.shape
    a = a.reshape(t // SUBLANES, SUBLANES, c)
    b = b.reshape(t // SUBLANES, SUBLANES, c)
    row = lax.broadcasted_iota(I32, a.shape, 1)
    s = 1
    while s < SUBLANES:
        if reverse:
            m = row < (SUBLANES - s)
            shift = SUBLANES - s
        else:
            m = row >= s
            shift = s
        a_sh = jnp.where(m, pltpu.roll(a, shift, 1), 1.0)
        b_sh = jnp.where(m, pltpu.roll(b, shift, 1), 0.0)
        b = a * b_sh + b
        a = a * a_sh
        s *= 2
    return a.reshape(t, c), b.reshape(t, c)


def _rg_scan_kernel(u_ref, gg_ref, cw_ref, cb_ref, wg_ref, bg_ref, lam_ref, o_ref,
                    af_ref, bf_ref, ab_ref, bb_ref, *, chunk):
    s_len, c = u_ref.shape
    t = chunk
    n_chunks = s_len // t
    halo = SUBLANES
    lw = t + 2 * halo

    lam = -lam_ref[...]
    sp = jnp.maximum(lam, 0.0) + jnp.log1p(jnp.exp(-jnp.abs(lam)))
    decay = (-RG_C) * sp
    cw = cw_ref[...]
    cb = cb_ref[...]
    bg = bg_ref[...]

    def pass1(ci, _):
        t0 = pl.multiple_of(ci * t, t)
        lo = pl.multiple_of(jnp.maximum(t0 - halo, 0), halo)
        hi = pl.multiple_of(jnp.minimum(t0 + t, s_len - halo), halo)
        prev = jnp.where(ci > 0, u_ref[pl.ds(lo, halo), :], 0.0)
        nxt = jnp.where(ci < n_chunks - 1, u_ref[pl.ds(hi, halo), :], 0.0)
        w = jnp.concatenate([prev, u_ref[pl.ds(t0, t), :], nxt], axis=0)
        conv = (cw[0:1] * pltpu.roll(w, 2, 0) + cw[1:2] * pltpu.roll(w, 1, 0)
                + cw[2:3] * w + cw[3:4] * pltpu.roll(w, lw - 1, 0))
        conv = conv[halo:halo + t] + cb
        gates = jnp.dot(conv.astype(BF16), wg_ref[...], preferred_element_type=F32) + bg
        for d, (a_ref, b_ref) in enumerate(((af_ref, bf_ref), (ab_ref, bb_ref))):
            i_t = _sigmoid(gates[:, (2 * d) * c:(2 * d + 1) * c])
            r_t = _sigmoid(gates[:, (2 * d + 1) * c:(2 * d + 2) * c])
            log_a = r_t * decay[d:d + 1]
            a_t = jnp.exp(log_a)
            b_t = jnp.sqrt(-jnp.tanh(log_a) * (a_t * a_t + 1.0)) * (i_t * conv)
            a_s, b_s = _tile_scan(a_t, b_t, reverse=(d == 1))
            a_ref[pl.ds(t0, t), :] = a_s
            b_ref[pl.ds(t0, t), :] = b_s
        return 0

    lax.fori_loop(0, n_chunks, pass1, 0)

    n_tiles = s_len // SUBLANES

    def pass2(k, carry):
        hf, hb = carry
        rf = pl.multiple_of(k * SUBLANES, SUBLANES)
        rb = pl.multiple_of((n_tiles - 1 - k) * SUBLANES, SUBLANES)
        h_f = bf_ref[pl.ds(rf, SUBLANES), :] + af_ref[pl.ds(rf, SUBLANES), :] * hf
        bf_ref[pl.ds(rf, SUBLANES), :] = h_f
        h_b = bb_ref[pl.ds(rb, SUBLANES), :] + ab_ref[pl.ds(rb, SUBLANES), :] * hb
        bb_ref[pl.ds(rb, SUBLANES), :] = h_b
        return h_f[SUBLANES - 1:SUBLANES, :], h_b[0:1, :]

    zero = jnp.zeros((1, c), F32)
    lax.fori_loop(0, n_tiles, pass2, (zero, zero), unroll=8)

    def pass3(ci, _):
        t0 = pl.multiple_of(ci * t, t)
        o_ref[pl.ds(t0, t), :] = ((bf_ref[pl.ds(t0, t), :] + bb_ref[pl.ds(t0, t), :])
                                  * gg_ref[pl.ds(t0, t), :]).astype(o_ref.dtype)
        return 0

    lax.fori_loop(0, n_chunks, pass3, 0)


def _rg_scan(u, gg, conv_w_all, conv_b_all, wg_cat, bg_cat, lam_all, lj, batch, chunk):
    n, w = u.shape
    s_len = n // batch
    heads, c, c4 = wg_cat.shape
    kw = conv_w_all.shape[1]
    u3 = u.reshape(batch, s_len, w)
    gg3 = gg.reshape(batch, s_len, w)
    out = pl.pallas_call(
        functools.partial(_rg_scan_kernel, chunk=chunk),
        grid=(batch, heads),
        in_specs=[
            pl.BlockSpec((None, s_len, c), lambda b, h: (b, 0, h)),
            pl.BlockSpec((None, s_len, c), lambda b, h: (b, 0, h)),
            pl.BlockSpec((None, kw, c), lambda b, h: (lj, 0, h)),
            pl.BlockSpec((None, 1, c), lambda b, h: (lj, 0, h)),
            pl.BlockSpec((None, c, c4), lambda b, h: (h, 0, 0)),
            pl.BlockSpec((None, 1, c4), lambda b, h: (h, 0, 0)),
            pl.BlockSpec((None, 2, c), lambda b, h: (lj, 0, h)),
        ],
        out_specs=pl.BlockSpec((None, s_len, c), lambda b, h: (b, 0, h)),
        out_shape=jax.ShapeDtypeStruct((batch, s_len, w), BF16),
        scratch_shapes=[pltpu.VMEM((s_len, c), F32)] * 4,
        compiler_params=_cparams(("arbitrary", "arbitrary")),
        name="rg_scan",
    )(u3, gg3, conv_w_all, conv_b_all, wg_cat, bg_cat, lam_all)
    return out.reshape(n, w)


def _mix_out_kernel(y_ref, h_ref, w_ref, b_ref, g_ref, wr_ref, hout_ref, xn_ref, lt_ref, wbf_ref):
    @pl.when(pl.program_id(0) == 0)
    def _():
        _cast_rows(w_ref, wbf_ref)

    hn = (h_ref[...] + jnp.dot(y_ref[...].astype(BF16), wbf_ref[...], preferred_element_type=F32)
          + b_ref[...])
    hout_ref[...] = hn
    xn = _rms(hn, g_ref[...])
    xn_ref[...] = xn.astype(BF16)
    lt_ref[...] = lax.dot_general(wr_ref[...], xn, (((1,), (1,)), ((), ())),
                                  precision=lax.Precision.HIGHEST, preferred_element_type=F32)


def _mix_out(y, h, w_all, lj, b_all, g_all, li, w_router_t_all, tm):
    n, d = h.shape
    k = y.shape[1]
    e = w_router_t_all.shape[1]
    return pl.pallas_call(
        _mix_out_kernel,
        grid=(n // tm,),
        in_specs=[
            pl.BlockSpec((tm, k), lambda i: (i, 0)),
            pl.BlockSpec((tm, d), lambda i: (i, 0)),
            _layer_spec((k, d), lj, 1),
            _layer_spec((1, d), lj, 1),
            _layer_spec((1, d), li, 1),
            _layer_spec((e, d), li, 1),
        ],
        out_specs=[
            pl.BlockSpec((tm, d), lambda i: (i, 0)),
            pl.BlockSpec((tm, d), lambda i: (i, 0)),
            pl.BlockSpec((e, tm), lambda i: (0, i)),
        ],
        out_shape=[jax.ShapeDtypeStruct((n, d), F32), jax.ShapeDtypeStruct((n, d), BF16),
                   jax.ShapeDtypeStruct((e, n), F32)],
        scratch_shapes=[pltpu.VMEM((k, d), BF16)],
        compiler_params=_cparams(("arbitrary",)),
        name="mix_out",
    )(y, h, w_all, b_all, g_all, w_router_t_all)


def _excl_cumsum_lanes(x):
    s_len = x.shape[1]
    lane = lax.broadcasted_iota(I32, x.shape, 1)
    acc = x
    s = 1
    while s < s_len:
        acc = acc + jnp.where(lane >= s, pltpu.roll(acc, s, 1), 0)
        s *= 2
    return acc - x


def _route_kernel(lt_ref, ps_ref, gc_ref, *, cap):
    e, s_len = lt_ref.shape
    lg = lt_ref[...]
    ex = jnp.exp(lg - jnp.max(lg, axis=0, keepdims=True))
    aff = ex / jnp.sum(ex, axis=0, keepdims=True)

    def count(mask):
        return jnp.sum(mask.astype(F32), axis=1, keepdims=True).astype(I32)

    def radix(i, prefix):
        cand = prefix | lax.shift_left(jnp.int32(1), 30 - i)
        cand_f = lax.bitcast_convert_type(cand, F32)
        return jnp.where(count(aff >= cand_f) >= cap, cand, prefix)

    thr_bits = lax.fori_loop(0, 31, radix, jnp.zeros((e, 1), I32))
    thr = lax.bitcast_convert_type(thr_bits, F32)
    above = lax.bitcast_convert_type(thr_bits + 1, F32)
    gt = aff >= above
    eq = (aff >= thr) & jnp.logical_not(gt)
    need = cap - count(gt)
    eq_rank = _excl_cumsum_lanes(eq.astype(I32))
    sel_i = (gt | (eq & (eq_rank < need))).astype(I32)
    pos = _excl_cumsum_lanes(sel_i)
    ps_ref[...] = pos * 2 + sel_i

    rh = max(cap // GATE_LANES, BF16_ROWS)
    row_hi = lax.broadcasted_iota(I32, (rh, s_len), 0)
    row_lo = lax.broadcasted_iota(I32, (GATE_LANES, s_len), 0)
    a1 = aff.astype(BF16).astype(F32)
    a2 = (aff - a1).astype(BF16).astype(F32)
    a3 = (aff - a1 - a2).astype(BF16).astype(F32)
    slot_r = lax.broadcasted_iota(I32, (cap, rh), 0)
    pick_hi = _one_hot((slot_r >> GATE_SHIFT) == lax.broadcasted_iota(I32, (cap, rh), 1))
    pick_lo = ((lax.broadcasted_iota(I32, (cap, GATE_LANES), 0) & (GATE_LANES - 1))
               == lax.broadcasted_iota(I32, (cap, GATE_LANES), 1))
    for ei in range(e):
        pe = pos[ei:ei + 1, :]
        oh_hi = ((pe >> GATE_SHIFT) == row_hi) & (sel_i[ei:ei + 1, :] > 0)
        oh_lo = _one_hot((pe & (GATE_LANES - 1)) == row_lo)
        lhs = jnp.concatenate([jnp.where(oh_hi, a[ei:ei + 1, :], 0.0) for a in (a1, a2, a3)],
                              axis=0).astype(BF16)
        r = lax.dot_general(lhs, oh_lo, (((1,), (1,)), ((), ())), preferred_element_type=F32)
        r = r.astype(BF16)
        rows = sum(jnp.dot(pick_hi, r[k * rh:(k + 1) * rh], preferred_element_type=F32)
                   for k in range(3))
        gc_ref[ei] = jnp.sum(jnp.where(pick_lo, rows, 0.0), axis=1, keepdims=True)


def _route(lt, batch, cap):
    e, n = lt.shape
    s_len = n // batch
    assert cap % GATE_LANES == 0
    return pl.pallas_call(
        functools.partial(_route_kernel, cap=cap),
        grid=(batch,),
        in_specs=[pl.BlockSpec((e, s_len), lambda b: (0, b))],
        out_specs=[
            pl.BlockSpec((None, e, s_len), lambda b: (b, 0, 0)),
            pl.BlockSpec((None, e, cap, 1), lambda b: (b, 0, 0, 0)),
        ],
        out_shape=[
            jax.ShapeDtypeStruct((batch, e, s_len), I32),
            jax.ShapeDtypeStruct((batch, e, cap, 1), F32),
        ],
        compiler_params=_cparams(("arbitrary",)),
        name="route",
    )(lt)


def _window(start, q, wp, cap):
    u = (start & (-BF16_ROWS)) + q * wp
    return u, jnp.minimum(u, cap - wp)


def _dispatch_kernel(st_ref, nr_ref, xn_ref, ps_ref, out_ref, *, wp):
    b = pl.program_id(0)
    i = pl.program_id(1)
    tile = b * pl.num_programs(1) + i
    e, cap, d = out_ref.shape
    t = xn_ref.shape[0]

    @pl.when(i == 0)
    def _():
        def zero(ei, _):
            out_ref[ei] = jnp.zeros((cap, d), BF16)
            return 0

        lax.fori_loop(0, e, zero, 0)

    x = xn_ref[...]
    ps = ps_ref[...]
    pos = ps >> 1
    sel = (ps & 1) > 0
    start = pos[:, 0:1]
    sub = lax.broadcasted_iota(I32, (wp, t), 0)

    def one_round(q):
        u, c = _window(start, q, wp, cap)
        prel = jnp.where(sel & (pos >= u) & (pos < u + wp), pos - c, -1)
        oh = jnp.concatenate([_one_hot(sub == prel[ei:ei + 1, :]) for ei in range(e)], axis=0)
        win = jnp.dot(oh, x, preferred_element_type=F32)
        for ei in range(e):
            ws = _window(st_ref[tile * e + ei], q, wp, cap)[1]
            rows = pl.ds(pl.multiple_of(ws, BF16_ROWS), wp)
            out_ref[ei, rows, :] = out_ref[ei, rows, :] + win[ei * wp:(ei + 1) * wp].astype(BF16)

    one_round(0)

    def extra(q, _):
        one_round(q)
        return 0

    lax.fori_loop(1, nr_ref[tile], extra, 0)


def _dispatch(st_flat, nr, xn, psel, batch, cap, tt, wp):
    n, d = xn.shape
    s_len = n // batch
    e = psel.shape[1]
    nt = s_len // tt
    grid_spec = pltpu.PrefetchScalarGridSpec(
        num_scalar_prefetch=2,
        grid=(batch, nt),
        in_specs=[
            pl.BlockSpec((tt, d), lambda b, i, st, nr: (b * nt + i, 0)),
            pl.BlockSpec((None, e, tt), lambda b, i, st, nr: (b, 0, i)),
        ],
        out_specs=pl.BlockSpec((e, None, cap, d), lambda b, i, st, nr: (0, b, 0, 0)),
    )
    return pl.pallas_call(
        functools.partial(_dispatch_kernel, wp=wp),
        grid_spec=grid_spec,
        out_shape=jax.ShapeDtypeStruct((e, batch, cap, d), BF16),
        compiler_params=_cparams(("arbitrary", "arbitrary")),
        name="dispatch",
    )(st_flat, nr, xn, psel)


def _ffn_kernel(xg_ref, gc_ref, wg_ref, wu_ref, wd_ref, out_ref, wgub_ref, wdb_ref, acc_ref):
    fh = pl.program_id(1)
    batch, cap, _ = xg_ref.shape

    fw = wg_ref.shape[1]
    _cast_rows(wg_ref, wgub_ref.at[:, pl.ds(0, fw)], unroll=True)
    _cast_rows(wu_ref, wgub_ref.at[:, pl.ds(fw, fw)], unroll=True)
    _cast_rows(wd_ref, wdb_ref, unroll=True)

    @pl.when((pl.program_id(0) == 0) & (fh == 0))
    def _():
        acc_ref[...] = jnp.zeros(acc_ref.shape, F32)

    nb = 2 if batch % 2 == 0 else 1
    for b in range(0, batch, nb):
        x = xg_ref[pl.ds(b, nb)].reshape(nb * cap, xg_ref.shape[2])
        hgu = jnp.dot(x, wgub_ref[...], preferred_element_type=F32)
        hgate, hup = hgu[:, :fw], hgu[:, fw:]
        hg = (hgate * _sigmoid(hgate) * hup).astype(BF16)
        y = jnp.dot(hg, wdb_ref[...], preferred_element_type=F32)
        rows = pl.ds(b * cap, nb * cap)
        acc_ref[rows, :] = jnp.where(fh > 0, acc_ref[rows, :], 0.0) + y

    @pl.when(fh == pl.num_programs(1) - 1)
    def _():
        def scale(i, _):
            rows = pl.ds(pl.multiple_of(i * cap, cap), cap)
            out_ref[rows, :] = (acc_ref[rows, :] * gc_ref[i]).astype(BF16)
            return 0

        lax.fori_loop(0, batch, scale, 0)


def _ffn(xg, gc, w_gate_all, w_up_all, w_down_all, li, n_fh):
    e, batch, cap, d = xg.shape
    f = w_gate_all.shape[3]
    fh = f // n_fh
    return pl.pallas_call(
        _ffn_kernel,
        grid=(e, n_fh),
        in_specs=[
            pl.BlockSpec((None, batch, cap, d), lambda ei, fi: (ei, 0, 0, 0)),
            pl.BlockSpec((batch, None, cap, 1), lambda ei, fi: (0, ei, 0, 0)),
            pl.BlockSpec((None, None, d, fh), lambda ei, fi: (li, ei, 0, fi)),
            pl.BlockSpec((None, None, d, fh), lambda ei, fi: (li, ei, 0, fi)),
            pl.BlockSpec((None, None, fh, d), lambda ei, fi: (li, ei, fi, 0)),
        ],
        out_specs=pl.BlockSpec((None, batch * cap, d), lambda ei, fi: (ei, 0, 0)),
        out_shape=jax.ShapeDtypeStruct((e, batch * cap, d), BF16),
        scratch_shapes=[
            pltpu.VMEM((d, 2 * fh), BF16),
            pltpu.VMEM((fh, d), BF16),
            pltpu.VMEM((batch * cap, d), F32),
        ],
        compiler_params=_cparams(("arbitrary", "arbitrary")),
        name="expert_ffn",
    )(xg, gc, w_gate_all, w_up_all, w_down_all)


def _combine_kernel(st_ref, nr_ref, h_ref, ps_ref, p_ref, yg_hbm, wp_ref, wgt_ref,
                    gple_ref, gfin_ref, hout_ref, slab_ref, acc_ref, wpb_ref, wgb_ref, sem,
                    *, final, wp, cap, batch, tiles_per_batch):
    i = pl.program_id(0)
    t, e = ps_ref.shape
    slot = i & 1

    def window_copies(tile, q, buf):
        b = tile // tiles_per_batch
        copies = []
        for ei in range(e):
            ws = _window(st_ref[tile * e + ei], q, wp, cap)[1]
            row0 = pl.multiple_of((ei * batch + b) * cap + ws, BF16_ROWS)
            copies.append(pltpu.make_async_copy(
                yg_hbm.at[pl.ds(row0, wp), :], slab_ref.at[buf, pl.ds(ei * wp, wp), :],
                sem.at[buf]))
        return copies

    @pl.when(i == 0)
    def _():
        _cast_rows(wp_ref, wpb_ref)
        _cast_rows(wgt_ref, wgb_ref)
        for c in window_copies(0, 0, 0):
            c.start()

    @pl.when(i + 1 < pl.num_programs(0))
    def _():
        for c in window_copies(i + 1, 0, 1 - slot):
            c.start()

    ps = ps_ref[...]
    pos = ps >> 1
    sel = (ps & 1) > 0
    start = pos[0:1, :]
    k = e * wp
    wp_shift = wp.bit_length() - 1
    expand = _one_hot(lax.broadcasted_iota(I32, (e, k), 1) >> wp_shift
                      == lax.broadcasted_iota(I32, (e, k), 0))
    lane_w = (lax.broadcasted_iota(I32, (t, k), 1) & (wp - 1)).astype(F32)

    def apply_round(q):
        u, c = _window(start, q, wp, cap)
        prel = jnp.where(sel & (pos >= u) & (pos < u + wp), pos - c, -1)
        pexp = jnp.dot(prel.astype(F32).astype(BF16), expand, preferred_element_type=F32)
        oh = _one_hot(pexp == lane_w)
        acc_ref[...] += jnp.dot(oh, slab_ref[slot], preferred_element_type=F32)

    acc_ref[...] = h_ref[...]
    for c in window_copies(i, 0, slot):
        c.wait()
    apply_round(0)

    def extra(q, _):
        for c in window_copies(i, q, slot):
            c.start()
        for c in window_copies(i, q, slot):
            c.wait()
        apply_round(q)
        return 0

    lax.fori_loop(1, nr_ref[i], extra, 0)

    hn = acc_ref[...]
    gate = _sigmoid(jnp.dot(_rms(hn, gple_ref[...]).astype(BF16), wgb_ref[...],
                            preferred_element_type=F32))
    hn = hn + gate * jnp.dot(p_ref[...].astype(BF16), wpb_ref[...], preferred_element_type=F32)
    if final:
        hn = _rms(hn, gfin_ref[...])
    hout_ref[...] = hn


def _combine(st_flat, nr, h, psel_tm, p_all, yg, w_proj_all, w_gate_all, g_ple_all, li, g_final,
             batch, cap, tt, wp, final):
    n, d = h.shape
    e = psel_tm.shape[1]
    pd = p_all.shape[2]
    nt = n // tt
    grid_spec = pltpu.PrefetchScalarGridSpec(
        num_scalar_prefetch=2,
        grid=(nt,),
        in_specs=[
            pl.BlockSpec((tt, d), lambda i, st, nr: (i, 0)),
            pl.BlockSpec((tt, e), lambda i, st, nr: (i, 0)),
            pl.BlockSpec((None, tt, pd), lambda i, st, nr: (li, i, 0)),
            pl.BlockSpec(memory_space=pl.ANY),
            pl.BlockSpec((None, pd, d), lambda i, st, nr: (li, 0, 0)),
            pl.BlockSpec((None, d, d), lambda i, st, nr: (li, 0, 0)),
            pl.BlockSpec((None, 1, d), lambda i, st, nr: (li, 0, 0)),
            pl.BlockSpec((1, d), lambda i, st, nr: (0, 0)),
        ],
        out_specs=pl.BlockSpec((tt, d), lambda i, st, nr: (i, 0)),
        scratch_shapes=[
            pltpu.VMEM((2, e * wp, d), BF16),
            pltpu.VMEM((tt, d), F32),
            pltpu.VMEM((pd, d), BF16),
            pltpu.VMEM((d, d), BF16),
            pltpu.SemaphoreType.DMA((2,)),
        ],
    )
    return pl.pallas_call(
        functools.partial(_combine_kernel, final=final, wp=wp, cap=cap, batch=batch,
                          tiles_per_batch=nt // batch),
        grid_spec=grid_spec,
        out_shape=jax.ShapeDtypeStruct((n, d), F32),
        compiler_params=_cparams(("arbitrary",)),
        name="combine_ple",
    )(st_flat, nr, h, psel_tm, p_all, yg, w_proj_all, w_gate_all, g_ple_all,
      g_final.reshape(1, d))


RADIX = 16


def _fourier_in_kernel(h_ref, g_ref, w_ref, cs_ref, out_ref, wbf_ref, *, groups):
    @pl.when((pl.program_id(0) == 0) & (pl.program_id(1) == 0))
    def _():
        _cast_rows(w_ref, wbf_ref)

    xn = _rms(h_ref[...], g_ref[...])
    u = jnp.dot(xn.astype(BF16), wbf_ref[...], preferred_element_type=F32).astype(BF16)
    gd = u.shape[1] // groups
    for gi in range(groups):
        r = jnp.dot(u[:, gi * gd:(gi + 1) * gd], cs_ref[...], preferred_element_type=F32)
        out_ref[0, :, gi * gd:(gi + 1) * gd] = r[:, :gd].astype(BF16)
        out_ref[1, :, gi * gd:(gi + 1) * gd] = r[:, gd:].astype(BF16)


def _fourier_in(h, g_all, li, w_in_all, lj, cs_tab, batch, tm):
    n, d = h.shape
    s_len = n // batch
    nt = s_len // tm
    return pl.pallas_call(
        functools.partial(_fourier_in_kernel, groups=FOURIER_GROUPS),
        grid=(batch, nt),
        in_specs=[
            pl.BlockSpec((tm, d), lambda b, i: (b * nt + i, 0)),
            _layer_spec((1, d), li, 2),
            _layer_spec((d, d), lj, 2),
            pl.BlockSpec(cs_tab.shape, lambda b, i: (0, 0)),
        ],
        out_specs=pl.BlockSpec((None, 2, tm, d), lambda b, i: (b, 0, i, 0)),
        out_shape=jax.ShapeDtypeStruct((batch, 2, s_len, d), BF16),
        scratch_shapes=[pltpu.VMEM((d, d), BF16)],
        compiler_params=_cparams(("arbitrary", "arbitrary")),
        name="fourier_in",
    )(h, g_all, w_in_all, cs_tab)


def _seq_fft_kernel(y_ref, tp_ref, ta_ref, tb_ref, out_ref, ys_ref, scr_ref, *, scale):
    radix = ta_ref.shape[0]
    _, s_len, dc = y_ref.shape
    n_a = s_len // radix
    g = BF16_ROWS
    n_m = n_a // g
    rg = radix * g

    def regroup(m, _):
        base = pl.multiple_of(m * rg, rg)
        dst = pl.ds(pl.multiple_of(m * g, g), g)
        for part in range(2):
            yp = jnp.dot(tp_ref[...], y_ref[part, pl.ds(base, rg), :],
                         preferred_element_type=F32).astype(BF16)
            for n2 in range(radix):
                ys_ref[n2, part, dst, :] = yp[n2 * g:(n2 + 1) * g]
        return 0

    lax.fori_loop(0, n_m, regroup, 0, unroll=8)

    def stage_a(n2, _):
        y = ys_ref[n2].reshape(2 * n_a, dc)
        a = jnp.dot(ta_ref[n2], y, preferred_element_type=F32).astype(BF16)
        off = pl.multiple_of(n2 * g, g)
        for m in range(n_m):
            scr_ref[m, pl.ds(off, g), :] = a[m * g:(m + 1) * g]
            scr_ref[m, pl.ds(rg + off, g), :] = a[n_a + m * g:n_a + (m + 1) * g]
        return 0

    lax.fori_loop(0, radix, stage_a, 0, unroll=8)

    def stage_b(m, _):
        o = (jnp.dot(tb_ref[...], scr_ref[m], preferred_element_type=F32) * scale
             ).astype(out_ref.dtype)
        base = pl.multiple_of(m * g, g)
        for k2 in range(radix):
            out_ref[pl.ds(k2 * n_a + base, g), :] = o[k2 * g:(k2 + 1) * g]
        return 0

    lax.fori_loop(0, n_m, stage_b, 0, unroll=8)


def _seq_fft(ucs, tab_p, tab_a, tab_b, scale, dc):
    batch, _, s_len, d = ucs.shape
    radix = tab_a.shape[0]
    n_a = s_len // radix
    assert n_a % BF16_ROWS == 0
    rg = radix * BF16_ROWS
    return pl.pallas_call(
        functools.partial(_seq_fft_kernel, scale=scale),
        grid=(batch, d // dc),
        in_specs=[
            pl.BlockSpec((None, 2, s_len, dc), lambda b, j: (b, 0, 0, j)),
            pl.BlockSpec(tab_p.shape, lambda b, j: (0, 0)),
            pl.BlockSpec(tab_a.shape, lambda b, j: (0, 0, 0)),
            pl.BlockSpec(tab_b.shape, lambda b, j: (0, 0)),
        ],
        out_specs=pl.BlockSpec((None, s_len, dc), lambda b, j: (b, 0, j)),
        out_shape=jax.ShapeDtypeStruct((batch, s_len, d), BF16),
        scratch_shapes=[pltpu.VMEM((radix, 2, n_a, dc), BF16),
                        pltpu.VMEM((n_a // BF16_ROWS, 2 * rg, dc), BF16)],
        compiler_params=_cparams(("arbitrary", "arbitrary")),
        name="seq_fft",
    )(ucs, tab_p, tab_a, tab_b)


def _dft_tables(s_len, gd):
    n_a = s_len // RADIX
    i = np.arange(gd, dtype=np.int64)
    ang = ((i[:, None] * i[None, :]) % gd).astype(np.float64) * (2.0 * np.pi / gd)
    ch = np.concatenate([np.cos(ang), np.sin(ang)], axis=1)

    k1 = np.arange(n_a, dtype=np.int64)[None, :, None]
    n1 = np.arange(n_a, dtype=np.int64)[None, None, :]
    n2 = np.arange(RADIX, dtype=np.int64)[:, None, None]
    ang = ((k1 * (RADIX * n1 + n2)) % s_len).astype(np.float64) * (2.0 * np.pi / s_len)
    c, sn = np.cos(ang), np.sin(ang)
    tab_a = np.concatenate([np.concatenate([c, -sn], axis=2),
                            np.concatenate([-sn, -c], axis=2)], axis=1)

    r = np.arange(RADIX, dtype=np.int64)
    ang = ((r[:, None] * r[None, :]) % RADIX).astype(np.float64) * (2.0 * np.pi / RADIX)
    eye = np.eye(BF16_ROWS)
    tab_b = np.concatenate([np.kron(np.cos(ang), eye), np.kron(np.sin(ang), eye)], axis=1)

    rg = RADIX * BF16_ROWS
    src = (np.arange(rg) % BF16_ROWS) * RADIX + np.arange(rg) // BF16_ROWS
    tab_p = np.zeros((rg, rg))
    tab_p[np.arange(rg), src] = 1.0
    as_bf16 = lambda a: jnp.asarray(a, dtype=F32).astype(BF16)
    return as_bf16(ch), as_bf16(tab_p), as_bf16(tab_a), as_bf16(tab_b)


def _window_rows(tt, cap, s_len):
    wp = min(max(2 * tt * cap // s_len, 2 * BF16_ROWS), cap)
    assert wp & (wp - 1) == 0
    return wp


def _tile_tables(psel, cap, tt, wp):
    batch, e, s_len = psel.shape
    st = (psel >> 1)[:, :, ::tt]
    nxt = jnp.concatenate([st[:, :, 1:], jnp.full((batch, e, 1), cap, I32)], axis=2)
    span = nxt - (st & (-BF16_ROWS))
    rounds = jnp.max((span + (wp - 1)) // wp, axis=1).reshape(-1)
    return jnp.swapaxes(st, 1, 2).reshape(-1), jnp.maximum(rounds, 1)


def kernel(x, p, g_mix, g_ffn, g_ple, g_final, rg_w_in, rg_b_in, rg_conv_w, rg_conv_b,
           rg_gx_w, rg_gx_b, rg_ga_w, rg_ga_b, rg_lam, rg_w_out, rg_b_out,
           ft_w_in, ft_w_out, w_router, w_gate, w_up, w_down, ple_w_proj, ple_w_gate):
    batch, s_len, d = x.shape
    n = batch * s_len
    depth = g_mix.shape[0]
    n_exp = w_router.shape[2]
    cap = max(1, CAPACITY_FACTOR * s_len // n_exp)
    gd = d // FOURIER_GROUPS
    tm = min(512, s_len)
    chunk = min(512, s_len)
    tt_d = tt_c = min(256, s_len)
    wp_d, wp_c = _window_rows(tt_d, cap, s_len), _window_rows(tt_c, cap, s_len)

    h = x.reshape(n, d)
    ch_tab, fft_tab_p, fft_tab_a, fft_tab_b = _dft_tables(s_len, gd)
    dft_scale = 1.0 / math.sqrt(s_len * gd)
    row = lambda a: a.reshape(a.shape[0], 1, a.shape[-1])
    g_mix3, g_ffn3, g_ple3 = row(g_mix), row(g_ffn), row(g_ple)
    w_router_t = jnp.swapaxes(w_router, 1, 2)
    p3 = p.reshape(depth, n, p.shape[-1])
    rg_b_in3, rg_conv_b3, rg_b_out3 = row(rg_b_in), row(rg_conv_b), row(rg_b_out)
    ft_b_out3 = jnp.zeros((ft_w_out.shape[0], 1, d), F32)

    for i in range(depth):
        j = i // 2
        if i % 2 == 0:
            gg, u = _rg_in(h, g_mix3, i, rg_w_in, rg_b_in3, j, tm)
            gx, ga = rg_gx_w[j], rg_ga_w[j]
            wg_cat = jnp.concatenate([gx[0], ga[0], gx[1], ga[1]], axis=-1).astype(BF16)
            gxb, gab = rg_gx_b[j], rg_ga_b[j]
            bg_cat = jnp.concatenate([gxb[0], gab[0], gxb[1], gab[1]], axis=-1)[:, None, :]
            y = _rg_scan(u, gg, rg_conv_w, rg_conv_b3, wg_cat, bg_cat, rg_lam, j, batch, chunk)
            h, xn, lt = _mix_out(y, h, rg_w_out, j, rg_b_out3, g_ffn3, i, w_router_t, tm)
        else:
            ucs = _fourier_in(h, g_mix3, i, ft_w_in, j, ch_tab, batch, tm)
            f = _seq_fft(ucs, fft_tab_p, fft_tab_a, fft_tab_b, dft_scale, min(256, d))
            h, xn, lt = _mix_out(f.reshape(n, d), h, ft_w_out, j, ft_b_out3, g_ffn3, i,
                                 w_router_t, tm)

        psel, gc = _route(lt, batch, cap)
        st_d, nr_d = _tile_tables(psel, cap, tt_d, wp_d)
        st_c, nr_c = _tile_tables(psel, cap, tt_c, wp_c)
        psel_tm = jnp.swapaxes(psel, 1, 2).reshape(n, n_exp)
        xg = _dispatch(st_d, nr_d, xn, psel, batch, cap, tt_d, wp_d)
        yg = _ffn(xg, gc, w_gate, w_up, w_down, i, 4)
        h = _combine(st_c, nr_c, h, psel_tm, p3, yg.reshape(n_exp * batch * cap, d),
                     ple_w_proj, ple_w_gate, g_ple3, i, g_final, batch, cap, tt_c,
                     wp_c, final=(i == depth - 1))
    return h.reshape(batch, s_len, d)
```

```python
import functools
import math

import jax
import jax.numpy as jnp
import numpy as np
from jax import lax
from jax.experimental import pallas as pl
from jax.experimental.pallas import tpu as pltpu

EPS = 1e-6
RG_C = 8.0
FOURIER_GROUPS = 4
CAPACITY_FACTOR = 2

F32 = jnp.float32
BF16 = jnp.bfloat16
I32 = jnp.int32

SUBLANES = 8
BF16_ROWS = 16
GATE_LANES = 32
GATE_SHIFT = 5
VMEM_LIMIT = 56 * 1024 * 1024


def _cparams(sem):
    return pltpu.CompilerParams(dimension_semantics=sem, vmem_limit_bytes=VMEM_LIMIT)


def _layer_spec(tail, li, nargs):
    zeros = (0,) * len(tail)
    if nargs == 1:
        return pl.BlockSpec((None,) + tail, lambda i: (li,) + zeros)
    if nargs == 2:
        return pl.BlockSpec((None,) + tail, lambda i, j: (li,) + zeros)
    return pl.BlockSpec((None,) + tail, lambda i, j, k: (li,) + zeros)


def _rms(x, g):
    return x * lax.rsqrt(jnp.mean(x * x, axis=-1, keepdims=True) + EPS) * g


def _gelu_tanh(x):
    c = math.sqrt(2.0 / math.pi)
    return 0.5 * x * (1.0 + jnp.tanh(c * (x + 0.044715 * (x * x * x))))


def _sigmoid(x):
    return 0.5 * jnp.tanh(0.5 * x) + 0.5


def _one_hot(mask):
    return jnp.where(mask, 1.0, 0.0).astype(BF16)


def _cast_rows(src_ref, dst_ref, rows=256, unroll=False):
    n = src_ref.shape[0]
    rows = min(rows, n)
    assert n % rows == 0
    if unroll:
        for r in range(0, n, rows):
            dst_ref[pl.ds(r, rows), :] = src_ref[pl.ds(r, rows), :].astype(dst_ref.dtype)
        return

    def body(i, _):
        r = pl.multiple_of(i * rows, rows)
        dst_ref[pl.ds(r, rows), :] = src_ref[pl.ds(r, rows), :].astype(dst_ref.dtype)
        return 0

    lax.fori_loop(0, n // rows, body, 0)


def _rg_in_kernel(h_ref, g_ref, w_ref, b_ref, gg_ref, u_ref, wbf_ref):
    @pl.when(pl.program_id(0) == 0)
    def _():
        _cast_rows(w_ref, wbf_ref)

    xn = _rms(h_ref[...], g_ref[...])
    proj = jnp.dot(xn.astype(BF16), wbf_ref[...], preferred_element_type=F32) + b_ref[...]
    w = gg_ref.shape[-1]
    gg_ref[...] = _gelu_tanh(proj[:, :w])
    u_ref[...] = proj[:, w:]


def _rg_in(h, g_all, li, w_in_all, b_in_all, lj, tm):
    n, d = h.shape
    w2 = w_in_all.shape[2]
    w = w2 // 2
    return pl.pallas_call(
        _rg_in_kernel,
        grid=(n // tm,),
        in_specs=[
            pl.BlockSpec((tm, d), lambda i: (i, 0)),
            _layer_spec((1, d), li, 1),
            _layer_spec((d, w2), lj, 1),
            _layer_spec((1, w2), lj, 1),
        ],
        out_specs=[
            pl.BlockSpec((tm, w), lambda i: (i, 0)),
            pl.BlockSpec((tm, w), lambda i: (i, 0)),
        ],
        out_shape=[jax.ShapeDtypeStruct((n, w), F32), jax.ShapeDtypeStruct((n, w), F32)],
        scratch_shapes=[pltpu.VMEM((d, w2), BF16)],
        compiler_params=_cparams(("arbitrary",)),
        name="rg_in",
    )(h, g_all, w_in_all, b_in_all)


def _tile_scan(a, b, reverse):
    t, c = a.shape
    a = a.reshape(t // SUBLANES, SUBLANES, c)
    b = b.reshape(t // SUBLANES, SUBLANES, c)
    row = lax.broadcasted_iota(I32, a.shape, 1)
    s = 1
    while s < SUBLANES:
        if reverse:
            m = row < (SUBLANES - s)
            shift = SUBLANES - s
        else:
            m = row >= s
            shift = s
        a_sh = jnp.where(m, pltpu.roll(a, shift, 1), 1.0)
        b_sh = jnp.where(m, pltpu.roll(b, shift, 1), 0.0)
        b = a * b_sh + b
        a = a * a_sh
        s *= 2
    return a.reshape(t, c), b.reshape(t, c)


def _rg_scan_kernel(u_ref, gg_ref, cw_ref, cb_ref, wg_ref, bg_ref, lam_ref, o_ref,
                    af_ref, bf_ref, ab_ref, bb_ref, *, chunk):
    s_len, c = u_ref.shape
    t = chunk
    n_chunks = s_len // t
    halo = SUBLANES
    lw = t + 2 * halo

    lam = -lam_ref[...]
    sp = jnp.maximum(lam, 0.0) + jnp.log1p(jnp.exp(-jnp.abs(lam)))
    decay = (-RG_C) * sp
    cw = cw_ref[...]
    cb = cb_ref[...]
    bg = bg_ref[...]

    def pass1(ci, _):
        t0 = pl.multiple_of(ci * t, t)
        lo = pl.multiple_of(jnp.maximum(t0 - halo, 0), halo)
        hi = pl.multiple_of(jnp.minimum(t0 + t, s_len - halo), halo)
        prev = jnp.where(ci > 0, u_ref[pl.ds(lo, halo), :], 0.0)
        nxt = jnp.where(ci < n_chunks - 1, u_ref[pl.ds(hi, halo), :], 0.0)
        w = jnp.concatenate([prev, u_ref[pl.ds(t0, t), :], nxt], axis=0)
        conv = (cw[0:1] * pltpu.roll(w, 2, 0) + cw[1:2] * pltpu.roll(w, 1, 0)
                + cw[2:3] * w + cw[3:4] * pltpu.roll(w, lw - 1, 0))
        conv = conv[halo:halo + t] + cb
        gates = jnp.dot(conv.astype(BF16), wg_ref[...], preferred_element_type=F32) + bg
        for d, (a_ref, b_ref) in enumerate(((af_ref, bf_ref), (ab_ref, bb_ref))):
            i_t = _sigmoid(gates[:, (2 * d) * c:(2 * d + 1) * c])
            r_t = _sigmoid(gates[:, (2 * d + 1) * c:(2 * d + 2) * c])
            log_a = r_t * decay[d:d + 1]
            a_t = jnp.exp(log_a)
            b_t = jnp.sqrt(-jnp.tanh(log_a) * (a_t * a_t + 1.0)) * (i_t * conv)
            a_s, b_s = _tile_scan(a_t, b_t, reverse=(d == 1))
            a_ref[pl.ds(t0, t), :] = a_s
            b_ref[pl.ds(t0, t), :] = b_s
        return 0

    lax.fori_loop(0, n_chunks, pass1, 0)

    n_tiles = s_len // SUBLANES

    def pass2(k, carry):
        hf, hb = carry
        rf = pl.multiple_of(k * SUBLANES, SUBLANES)
        rb = pl.multiple_of((n_tiles - 1 - k) * SUBLANES, SUBLANES)
        h_f = bf_ref[pl.ds(rf, SUBLANES), :] + af_ref[pl.ds(rf, SUBLANES), :] * hf
        bf_ref[pl.ds(rf, SUBLANES), :] = h_f
        h_b = bb_ref[pl.ds(rb, SUBLANES), :] + ab_ref[pl.ds(rb, SUBLANES), :] * hb
        bb_ref[pl.ds(rb, SUBLANES), :] = h_b
        return h_f[SUBLANES - 1:SUBLANES, :], h_b[0:1, :]

    zero = jnp.zeros((1, c), F32)
    lax.fori_loop(0, n_tiles, pass2, (zero, zero), unroll=8)

    def pass3(ci, _):
        t0 = pl.multiple_of(ci * t, t)
        o_ref[pl.ds(t0, t), :] = ((bf_ref[pl.ds(t0, t), :] + bb_ref[pl.ds(t0, t), :])
                                  * gg_ref[pl.ds(t0, t), :]).astype(o_ref.dtype)
        return 0

    lax.fori_loop(0, n_chunks, pass3, 0)


def _rg_scan(u, gg, conv_w_all, conv_b_all, wg_cat, bg_cat, lam_all, lj, batch, chunk):
    n, w = u.shape
    s_len = n // batch
    heads, c, c4 = wg_cat.shape
    kw = conv_w_all.shape[1]
    u3 = u.reshape(batch, s_len, w)
    gg3 = gg.reshape(batch, s_len, w)
    out = pl.pallas_call(
        functools.partial(_rg_scan_kernel, chunk=chunk),
        grid=(batch, heads),
        in_specs=[
            pl.BlockSpec((None, s_len, c), lambda b, h: (b, 0, h)),
            pl.BlockSpec((None, s_len, c), lambda b, h: (b, 0, h)),
            pl.BlockSpec((None, kw, c), lambda b, h: (lj, 0, h)),
            pl.BlockSpec((None, 1, c), lambda b, h: (lj, 0, h)),
            pl.BlockSpec((None, c, c4), lambda b, h: (h, 0, 0)),
            pl.BlockSpec((None, 1, c4), lambda b, h: (h, 0, 0)),
            pl.BlockSpec((None, 2, c), lambda b, h: (lj, 0, h)),
        ],
        out_specs=pl.BlockSpec((None, s_len, c), lambda b, h: (b, 0, h)),
        out_shape=jax.ShapeDtypeStruct((batch, s_len, w), BF16),
        scratch_shapes=[pltpu.VMEM((s_len, c), F32)] * 4,
        compiler_params=_cparams(("arbitrary", "arbitrary")),
        name="rg_scan",
    )(u3, gg3, conv_w_all, conv_b_all, wg_cat, bg_cat, lam_all)
    return out.reshape(n, w)


def _mix_out_kernel(y_ref, h_ref, w_ref, b_ref, g_ref, wr_ref, hout_ref, xn_ref, lt_ref, wbf_ref):
    @pl.when(pl.program_id(0) == 0)
    def _():
        _cast_rows(w_ref, wbf_ref)

    hn = (h_ref[...] + jnp.dot(y_ref[...].astype(BF16), wbf_ref[...], preferred_element_type=F32)
          + b_ref[...])
    hout_ref[...] = hn
    xn = _rms(hn, g_ref[...])
    xn_ref[...] = xn.astype(BF16)
    lt_ref[...] = lax.dot_general(wr_ref[...], xn, (((1,), (1,)), ((), ())),
                                  precision=lax.Precision.HIGHEST, preferred_element_type=F32)


def _mix_out(y, h, w_all, lj, b_all, g_all, li, w_router_t_all, tm):
    n, d = h.shape
    k = y.shape[1]
    e = w_router_t_all.shape[1]
    return pl.pallas_call(
        _mix_out_kernel,
        grid=(n // tm,),
        in_specs=[
            pl.BlockSpec((tm, k), lambda i: (i, 0)),
            pl.BlockSpec((tm, d), lambda i: (i, 0)),
            _layer_spec((k, d), lj, 1),
            _layer_spec((1, d), lj, 1),
            _layer_spec((1, d), li, 1),
            _layer_spec((e, d), li, 1),
        ],
        out_specs=[
            pl.BlockSpec((tm, d), lambda i: (i, 0)),
            pl.BlockSpec((tm, d), lambda i: (i, 0)),
            pl.BlockSpec((e, tm), lambda i: (0, i)),
        ],
        out_shape=[jax.ShapeDtypeStruct((n, d), F32), jax.ShapeDtypeStruct((n, d), BF16),
                   jax.ShapeDtypeStruct((e, n), F32)],
        scratch_shapes=[pltpu.VMEM((k, d), BF16)],
        compiler_params=_cparams(("arbitrary",)),
        name="mix_out",
    )(y, h, w_all, b_all, g_all, w_router_t_all)


def _excl_cumsum_lanes(x):
    s_len = x.shape[1]
    lane = lax.broadcasted_iota(I32, x.shape, 1)
    acc = x
    s = 1
    while s < s_len:
        acc = acc + jnp.where(lane >= s, pltpu.roll(acc, s, 1), 0)
        s *= 2
    return acc - x


def _route_kernel(lt_ref, ps_ref, gc_ref, *, cap):
    e, s_len = lt_ref.shape
    lg = lt_ref[...]
    ex = jnp.exp(lg - jnp.max(lg, axis=0, keepdims=True))
    aff = ex / jnp.sum(ex, axis=0, keepdims=True)

    def count(mask):
        return jnp.sum(mask.astype(F32), axis=1, keepdims=True).astype(I32)

    def radix(i, prefix):
        cand = prefix | lax.shift_left(jnp.int32(1), 30 - i)
        cand_f = lax.bitcast_convert_type(cand, F32)
        return jnp.where(count(aff >= cand_f) >= cap, cand, prefix)

    thr_bits = lax.fori_loop(0, 31, radix, jnp.zeros((e, 1), I32))
    thr = lax.bitcast_convert_type(thr_bits, F32)
    above = lax.bitcast_convert_type(thr_bits + 1, F32)
    gt = aff >= above
    eq = (aff >= thr) & jnp.logical_not(gt)
    need = cap - count(gt)
    eq_rank = _excl_cumsum_lanes(eq.astype(I32))
    sel_i = (gt | (eq & (eq_rank < need))).astype(I32)
    pos = _excl_cumsum_lanes(sel_i)
    ps_ref[...] = pos * 2 + sel_i

    rh = max(cap // GATE_LANES, BF16_ROWS)
    row_hi = lax.broadcasted_iota(I32, (rh, s_len), 0)
    row_lo = lax.broadcasted_iota(I32, (GATE_LANES, s_len), 0)
    a1 = aff.astype(BF16).astype(F32)
    a2 = (aff - a1).astype(BF16).astype(F32)
    a3 = (aff - a1 - a2).astype(BF16).astype(F32)
    slot_r = lax.broadcasted_iota(I32, (cap, rh), 0)
    pick_hi = _one_hot((slot_r >> GATE_SHIFT) == lax.broadcasted_iota(I32, (cap, rh), 1))
    pick_lo = ((lax.broadcasted_iota(I32, (cap, GATE_LANES), 0) & (GATE_LANES - 1))
               == lax.broadcasted_iota(I32, (cap, GATE_LANES), 1))
    for ei in range(e):
        pe = pos[ei:ei + 1, :]
        oh_hi = ((pe >> GATE_SHIFT) == row_hi) & (sel_i[ei:ei + 1, :] > 0)
        oh_lo = _one_hot((pe & (GATE_LANES - 1)) == row_lo)
        lhs = jnp.concatenate([jnp.where(oh_hi, a[ei:ei + 1, :], 0.0) for a in (a1, a2, a3)],
                              axis=0).astype(BF16)
        r = lax.dot_general(lhs, oh_lo, (((1,), (1,)), ((), ())), preferred_element_type=F32)
        r = r.astype(BF16)
        rows = sum(jnp.dot(pick_hi, r[k * rh:(k + 1) * rh], preferred_element_type=F32)
                   for k in range(3))
        gc_ref[ei] = jnp.sum(jnp.where(pick_lo, rows, 0.0), axis=1, keepdims=True)


def _route(lt, batch, cap):
    e, n = lt.shape
    s_len = n // batch
    assert cap % GATE_LANES == 0
    return pl.pallas_call(
        functools.partial(_route_kernel, cap=cap),
        grid=(batch,),
        in_specs=[pl.BlockSpec((e, s_len), lambda b: (0, b))],
        out_specs=[
            pl.BlockSpec((None, e, s_len), lambda b: (b, 0, 0)),
            pl.BlockSpec((None, e, cap, 1), lambda b: (b, 0, 0, 0)),
        ],
        out_shape=[
            jax.ShapeDtypeStruct((batch, e, s_len), I32),
            jax.ShapeDtypeStruct((batch, e, cap, 1), F32),
        ],
        compiler_params=_cparams(("arbitrary",)),
        name="route",
    )(lt)


def _window(start, q, wp, cap):
    u = (start & (-BF16_ROWS)) + q * wp
    return u, jnp.minimum(u, cap - wp)


def _dispatch_kernel(st_ref, nr_ref, xn_ref, ps_ref, out_ref, *, wp):
    b = pl.program_id(0)
    i = pl.program_id(1)
    tile = b * pl.num_programs(1) + i
    e, cap, d = out_ref.shape
    t = xn_ref.shape[0]

    @pl.when(i == 0)
    def _():
        def zero(ei, _):
            out_ref[ei] = jnp.zeros((cap, d), BF16)
            return 0

        lax.fori_loop(0, e, zero, 0)

    x = xn_ref[...]
    ps = ps_ref[...]
    pos = ps >> 1
    sel = (ps & 1) > 0
    start = pos[:, 0:1]
    sub = lax.broadcasted_iota(I32, (wp, t), 0)

    def one_round(q):
        u, c = _window(start, q, wp, cap)
        prel = jnp.where(sel & (pos >= u) & (pos < u + wp), pos - c, -1)
        oh = jnp.concatenate([_one_hot(sub == prel[ei:ei + 1, :]) for ei in range(e)], axis=0)
        win = jnp.dot(oh, x, preferred_element_type=F32)
        for ei in range(e):
            ws = _window(st_ref[tile * e + ei], q, wp, cap)[1]
            rows = pl.ds(pl.multiple_of(ws, BF16_ROWS), wp)
            out_ref[ei, rows, :] = out_ref[ei, rows, :] + win[ei * wp:(ei + 1) * wp].astype(BF16)

    one_round(0)

    def extra(q, _):
        one_round(q)
        return 0

    lax.fori_loop(1, nr_ref[tile], extra, 0)


def _dispatch(st_flat, nr, xn, psel, batch, cap, tt, wp):
    n, d = xn.shape
    s_len = n // batch
    e = psel.shape[1]
    nt = s_len // tt
    grid_spec = pltpu.PrefetchScalarGridSpec(
        num_scalar_prefetch=2,
        grid=(batch, nt),
        in_specs=[
            pl.BlockSpec((tt, d), lambda b, i, st, nr: (b * nt + i, 0)),
            pl.BlockSpec((None, e, tt), lambda b, i, st, nr: (b, 0, i)),
        ],
        out_specs=pl.BlockSpec((e, None, cap, d), lambda b, i, st, nr: (0, b, 0, 0)),
    )
    return pl.pallas_call(
        functools.partial(_dispatch_kernel, wp=wp),
        grid_spec=grid_spec,
        out_shape=jax.ShapeDtypeStruct((e, batch, cap, d), BF16),
        compiler_params=_cparams(("arbitrary", "arbitrary")),
        name="dispatch",
    )(st_flat, nr, xn, psel)


def _ffn_kernel(xg_ref, gc_ref, wg_ref, wu_ref, wd_ref, out_ref, wgub_ref, wdb_ref, acc_ref):
    fh = pl.program_id(1)
    batch, cap, _ = xg_ref.shape

    fw = wg_ref.shape[1]
    _cast_rows(wg_ref, wgub_ref.at[:, pl.ds(0, fw)], unroll=True)
    _cast_rows(wu_ref, wgub_ref.at[:, pl.ds(fw, fw)], unroll=True)
    _cast_rows(wd_ref, wdb_ref, unroll=True)

    @pl.when((pl.program_id(0) == 0) & (fh == 0))
    def _():
        acc_ref[...] = jnp.zeros(acc_ref.shape, F32)

    nb = 2 if batch % 2 == 0 else 1
    for b in range(0, batch, nb):
        x = xg_ref[pl.ds(b, nb)].reshape(nb * cap, xg_ref.shape[2])
        hgu = jnp.dot(x, wgub_ref[...], preferred_element_type=F32)
        hgate, hup = hgu[:, :fw], hgu[:, fw:]
        hg = (hgate * _sigmoid(hgate) * hup).astype(BF16)
        y = jnp.dot(hg, wdb_ref[...], preferred_element_type=F32)
        rows = pl.ds(b * cap, nb * cap)
        acc_ref[rows, :] = jnp.where(fh > 0, acc_ref[rows, :], 0.0) + y

    @pl.when(fh == pl.num_programs(1) - 1)
    def _():
        def scale(i, _):
            rows = pl.ds(pl.multiple_of(i * cap, cap), cap)
            out_ref[rows, :] = (acc_ref[rows, :] * gc_ref[i]).astype(BF16)
            return 0

        lax.fori_loop(0, batch, scale, 0)


def _ffn(xg, gc, w_gate_all, w_up_all, w_down_all, li, n_fh):
    e, batch, cap, d = xg.shape
    f = w_gate_all.shape[3]
    fh = f // n_fh
    return pl.pallas_call(
        _ffn_kernel,
        grid=(e, n_fh),
        in_specs=[
            pl.BlockSpec((None, batch, cap, d), lambda ei, fi: (ei, 0, 0, 0)),
            pl.BlockSpec((batch, None, cap, 1), lambda ei, fi: (0, ei, 0, 0)),
            pl.BlockSpec((None, None, d, fh), lambda ei, fi: (li, ei, 0, fi)),
            pl.BlockSpec((None, None, d, fh), lambda ei, fi: (li, ei, 0, fi)),
            pl.BlockSpec((None, None, fh, d), lambda ei, fi: (li, ei, fi, 0)),
        ],
        out_specs=pl.BlockSpec((None, batch * cap, d), lambda ei, fi: (ei, 0, 0)),
        out_shape=jax.ShapeDtypeStruct((e, batch * cap, d), BF16),
        scratch_shapes=[
            pltpu.VMEM((d, 2 * fh), BF16),
            pltpu.VMEM((fh, d), BF16),
            pltpu.VMEM((batch * cap, d), F32),
        ],
        compiler_params=_cparams(("arbitrary", "arbitrary")),
        name="expert_ffn",
    )(xg, gc, w_gate_all, w_up_all, w_down_all)


def _combine_kernel(st_ref, nr_ref, h_ref, ps_ref, p_ref, yg_hbm, wp_ref, wgt_ref,
                    gple_ref, gfin_ref, hout_ref, slab_ref, acc_ref, wpb_ref, wgb_ref, sem,
                    *, final, wp, cap, batch, tiles_per_batch):
    i = pl.program_id(0)
    t, e = ps_ref.shape
    slot = i & 1

    def window_copies(tile, q, buf):
        b = tile // tiles_per_batch
        copies = []
        for ei in range(e):
            ws = _window(st_ref[tile * e + ei], q, wp, cap)[1]
            row0 = pl.multiple_of((ei * batch + b) * cap + ws, BF16_ROWS)
            copies.append(pltpu.make_async_copy(
                yg_hbm.at[pl.ds(row0, wp), :], slab_ref.at[buf, pl.ds(ei * wp, wp), :],
                sem.at[buf]))
        return copies

    @pl.when(i == 0)
    def _():
        _cast_rows(wp_ref, wpb_ref)
        _cast_rows(wgt_ref, wgb_ref)
        for c in window_copies(0, 0, 0):
            c.start()

    @pl.when(i + 1 < pl.num_programs(0))
    def _():
        for c in window_copies(i + 1, 0, 1 - slot):
            c.start()

    ps = ps_ref[...]
    pos = ps >> 1
    sel = (ps & 1) > 0
    start = pos[0:1, :]
    k = e * wp
    wp_shift = wp.bit_length() - 1
    expand = _one_hot(lax.broadcasted_iota(I32, (e, k), 1) >> wp_shift
                      == lax.broadcasted_iota(I32, (e, k), 0))
    lane_w = (lax.broadcasted_iota(I32, (t, k), 1) & (wp - 1)).astype(F32)

    def apply_round(q):
        u, c = _window(start, q, wp, cap)
        prel = jnp.where(sel & (pos >= u) & (pos < u + wp), pos - c, -1)
        pexp = jnp.dot(prel.astype(F32).astype(BF16), expand, preferred_element_type=F32)
        oh = _one_hot(pexp == lane_w)
        acc_ref[...] += jnp.dot(oh, slab_ref[slot], preferred_element_type=F32)

    acc_ref[...] = h_ref[...]
    for c in window_copies(i, 0, slot):
        c.wait()
    apply_round(0)

    def extra(q, _):
        for c in window_copies(i, q, slot):
            c.start()
        for c in window_copies(i, q, slot):
            c.wait()
        apply_round(q)
        return 0

    lax.fori_loop(1, nr_ref[i], extra, 0)

    hn = acc_ref[...]
    gate = _sigmoid(jnp.dot(_rms(hn, gple_ref[...]).astype(BF16), wgb_ref[...],
                            preferred_element_type=F32))
    hn = hn + gate * jnp.dot(p_ref[...].astype(BF16), wpb_ref[...], preferred_element_type=F32)
    if final:
        hn = _rms(hn, gfin_ref[...])
    hout_ref[...] = hn


def _combine(st_flat, nr, h, psel_tm, p_all, yg, w_proj_all, w_gate_all, g_ple_all, li, g_final,
             batch, cap, tt, wp, final):
    n, d = h.shape
    e = psel_tm.shape[1]
    pd = p_all.shape[2]
    nt = n // tt
    grid_spec = pltpu.PrefetchScalarGridSpec(
        num_scalar_prefetch=2,
        grid=(nt,),
        in_specs=[
            pl.BlockSpec((tt, d), lambda i, st, nr: (i, 0)),
            pl.BlockSpec((tt, e), lambda i, st, nr: (i, 0)),
            pl.BlockSpec((None, tt, pd), lambda i, st, nr: (li, i, 0)),
            pl.BlockSpec(memory_space=pl.ANY),
            pl.BlockSpec((None, pd, d), lambda i, st, nr: (li, 0, 0)),
            pl.BlockSpec((None, d, d), lambda i, st, nr: (li, 0, 0)),
            pl.BlockSpec((None, 1, d), lambda i, st, nr: (li, 0, 0)),
            pl.BlockSpec((1, d), lambda i, st, nr: (0, 0)),
        ],
        out_specs=pl.BlockSpec((tt, d), lambda i, st, nr: (i, 0)),
        scratch_shapes=[
            pltpu.VMEM((2, e * wp, d), BF16),
            pltpu.VMEM((tt, d), F32),
            pltpu.VMEM((pd, d), BF16),
            pltpu.VMEM((d, d), BF16),
            pltpu.SemaphoreType.DMA((2,)),
        ],
    )
    return pl.pallas_call(
        functools.partial(_combine_kernel, final=final, wp=wp, cap=cap, batch=batch,
                          tiles_per_batch=nt // batch),
        grid_spec=grid_spec,
        out_shape=jax.ShapeDtypeStruct((n, d), F32),
        compiler_params=_cparams(("arbitrary",)),
        name="combine_ple",
    )(st_flat, nr, h, psel_tm, p_all, yg, w_proj_all, w_gate_all, g_ple_all,
      g_final.reshape(1, d))


RADIX = 16


def _fourier_in_kernel(h_ref, g_ref, w_ref, cs_ref, out_ref, wbf_ref, *, groups):
    @pl.when((pl.program_id(0) == 0) & (pl.program_id(1) == 0))
    def _():
        _cast_rows(w_ref, wbf_ref)

    xn = _rms(h_ref[...], g_ref[...])
    u = jnp.dot(xn.astype(BF16), wbf_ref[...], preferred_element_type=F32).astype(BF16)
    gd = u.shape[1] // groups
    for gi in range(groups):
        r = jnp.dot(u[:, gi * gd:(gi + 1) * gd], cs_ref[...], preferred_element_type=F32)
        out_ref[0, :, gi * gd:(gi + 1) * gd] = r[:, :gd].astype(BF16)
        out_ref[1, :, gi * gd:(gi + 1) * gd] = r[:, gd:].astype(BF16)


def _fourier_in(h, g_all, li, w_in_all, lj, cs_tab, batch, tm):
    n, d = h.shape
    s_len = n // batch
    nt = s_len // tm
    return pl.pallas_call(
        functools.partial(_fourier_in_kernel, groups=FOURIER_GROUPS),
        grid=(batch, nt),
        in_specs=[
            pl.BlockSpec((tm, d), lambda b, i: (b * nt + i, 0)),
            _layer_spec((1, d), li, 2),
            _layer_spec((d, d), lj, 2),
            pl.BlockSpec(cs_tab.shape, lambda b, i: (0, 0)),
        ],
        out_specs=pl.BlockSpec((None, 2, tm, d), lambda b, i: (b, 0, i, 0)),
        out_shape=jax.ShapeDtypeStruct((batch, 2, s_len, d), BF16),
        scratch_shapes=[pltpu.VMEM((d, d), BF16)],
        compiler_params=_cparams(("arbitrary", "arbitrary")),
        name="fourier_in",
    )(h, g_all, w_in_all, cs_tab)


def _seq_fft_kernel(y_ref, tp_ref, ta_ref, tb_ref, out_ref, ys_ref, scr_ref, *, scale):
    radix = ta_ref.shape[0]
    _, s_len, dc = y_ref.shape
    n_a = s_len // radix
    g = BF16_ROWS
    n_m = n_a // g
    rg = radix * g

    def regroup(m, _):
        base = pl.multiple_of(m * rg, rg)
        dst = pl.ds(pl.multiple_of(m * g, g), g)
        for part in range(2):
            yp = jnp.dot(tp_ref[...], y_ref[part, pl.ds(base, rg), :],
                         preferred_element_type=F32).astype(BF16)
            for n2 in range(radix):
                ys_ref[n2, part, dst, :] = yp[n2 * g:(n2 + 1) * g]
        return 0

    lax.fori_loop(0, n_m, regroup, 0, unroll=8)

    def stage_a(n2, _):
        y = ys_ref[n2].reshape(2 * n_a, dc)
        a = jnp.dot(ta_ref[n2], y, preferred_element_type=F32).astype(BF16)
        off = pl.multiple_of(n2 * g, g)
        for m in range(n_m):
            scr_ref[m, pl.ds(off, g), :] = a[m * g:(m + 1) * g]
            scr_ref[m, pl.ds(rg + off, g), :] = a[n_a + m * g:n_a + (m + 1) * g]
        return 0

    lax.fori_loop(0, radix, stage_a, 0, unroll=8)

    def stage_b(m, _):
        o = (jnp.dot(tb_ref[...], scr_ref[m], preferred_element_type=F32) * scale
             ).astype(out_ref.dtype)
        base = pl.multiple_of(m * g, g)
        for k2 in range(radix):
            out_ref[pl.ds(k2 * n_a + base, g), :] = o[k2 * g:(k2 + 1) * g]
        return 0

    lax.fori_loop(0, n_m, stage_b, 0, unroll=8)


def _seq_fft(ucs, tab_p, tab_a, tab_b, scale, dc):
    batch, _, s_len, d = ucs.shape
    radix = tab_a.shape[0]
    n_a = s_len // radix
    assert n_a % BF16_ROWS == 0
    rg = radix * BF16_ROWS
    return pl.pallas_call(
        functools.partial(_seq_fft_kernel, scale=scale),
        grid=(batch, d // dc),
        in_specs=[
            pl.BlockSpec((None, 2, s_len, dc), lambda b, j: (b, 0, 0, j)),
            pl.BlockSpec(tab_p.shape, lambda b, j: (0, 0)),
            pl.BlockSpec(tab_a.shape, lambda b, j: (0, 0, 0)),
            pl.BlockSpec(tab_b.shape, lambda b, j: (0, 0)),
        ],
        out_specs=pl.BlockSpec((None, s_len, dc), lambda b, j: (b, 0, j)),
        out_shape=jax.ShapeDtypeStruct((batch, s_len, d), BF16),
        scratch_shapes=[pltpu.VMEM((radix, 2, n_a, dc), BF16),
                        pltpu.VMEM((n_a // BF16_ROWS, 2 * rg, dc), BF16)],
        compiler_params=_cparams(("arbitrary", "arbitrary")),
        name="seq_fft",
    )(ucs, tab_p, tab_a, tab_b)


def _dft_tables(s_len, gd):
    n_a = s_len // RADIX
    i = np.arange(gd, dtype=np.int64)
    ang = ((i[:, None] * i[None, :]) % gd).astype(np.float64) * (2.0 * np.pi / gd)
    ch = np.concatenate([np.cos(ang), np.sin(ang)], axis=1)

    k1 = np.arange(n_a, dtype=np.int64)[None, :, None]
    n1 = np.arange(n_a, dtype=np.int64)[None, None, :]
    n2 = np.arange(RADIX, dtype=np.int64)[:, None, None]
    ang = ((k1 * (RADIX * n1 + n2)) % s_len).astype(np.float64) * (2.0 * np.pi / s_len)
    c, sn = np.cos(ang), np.sin(ang)
    tab_a = np.concatenate([np.concatenate([c, -sn], axis=2),
                            np.concatenate([-sn, -c], axis=2)], axis=1)

    r = np.arange(RADIX, dtype=np.int64)
    ang = ((r[:, None] * r[None, :]) % RADIX).astype(np.float64) * (2.0 * np.pi / RADIX)
    eye = np.eye(BF16_ROWS)
    tab_b = np.concatenate([np.kron(np.cos(ang), eye), np.kron(np.sin(ang), eye)], axis=1)

    rg = RADIX * BF16_ROWS
    src = (np.arange(rg) % BF16_ROWS) * RADIX + np.arange(rg) // BF16_ROWS
    tab_p = np.zeros((rg, rg))
    tab_p[np.arange(rg), src] = 1.0
    as_bf16 = lambda a: jnp.asarray(a, dtype=F32).astype(BF16)
    return as_bf16(ch), as_bf16(tab_p), as_bf16(tab_a), as_bf16(tab_b)


def _window_rows(tt, cap, s_len):
    wp = min(max(2 * tt * cap // s_len, 2 * BF16_ROWS), cap)
    assert wp & (wp - 1) == 0
    return wp


def _tile_tables(psel, cap, tt, wp):
    batch, e, s_len = psel.shape
    st = (psel >> 1)[:, :, ::tt]
    nxt = jnp.concatenate([st[:, :, 1:], jnp.full((batch, e, 1), cap, I32)], axis=2)
    span = nxt - (st & (-BF16_ROWS))
    rounds = jnp.max((span + (wp - 1)) // wp, axis=1).reshape(-1)
    return jnp.swapaxes(st, 1, 2).reshape(-1), jnp.maximum(rounds, 1)


def kernel(x, p, g_mix, g_ffn, g_ple, g_final, rg_w_in, rg_b_in, rg_conv_w, rg_conv_b,
           rg_gx_w, rg_gx_b, rg_ga_w, rg_ga_b, rg_lam, rg_w_out, rg_b_out,
           ft_w_in, ft_w_out, w_router, w_gate, w_up, w_down, ple_w_proj, ple_w_gate):
    batch, s_len, d = x.shape
    n = batch * s_len
    depth = g_mix.shape[0]
    n_exp = w_router.shape[2]
    cap = max(1, CAPACITY_FACTOR * s_len // n_exp)
    gd = d // FOURIER_GROUPS
    tm = min(512, s_len)
    chunk = min(512, s_len)
    tt_d = tt_c = min(256, s_len)
    wp_d, wp_c = _window_rows(tt_d, cap, s_len), _window_rows(tt_c, cap, s_len)

    h = x.reshape(n, d)
    ch_tab, fft_tab_p, fft_tab_a, fft_tab_b = _dft_tables(s_len, gd)
    dft_scale = 1.0 / math.sqrt(s_len * gd)
    row = lambda a: a.reshape(a.shape[0], 1, a.shape[-1])
    g_mix3, g_ffn3, g_ple3 = row(g_mix), row(g_ffn), row(g_ple)
    w_router_t = jnp.swapaxes(w_router, 1, 2)
    p3 = p.reshape(depth, n, p.shape[-1])
    rg_b_in3, rg_conv_b3, rg_b_out3 = row(rg_b_in), row(rg_conv_b), row(rg_b_out)
    ft_b_out3 = jnp.zeros((ft_w_out.shape[0], 1, d), F32)

    for i in range(depth):
        j = i // 2
        if i % 2 == 0:
            gg, u = _rg_in(h, g_mix3, i, rg_w_in, rg_b_in3, j, tm)
            gx, ga = rg_gx_w[j], rg_ga_w[j]
            wg_cat = jnp.concatenate([gx[0], ga[0], gx[1], ga[1]], axis=-1).astype(BF16)
            gxb, gab = rg_gx_b[j], rg_ga_b[j]
            bg_cat = jnp.concatenate([gxb[0], gab[0], gxb[1], gab[1]], axis=-1)[:, None, :]
            y = _rg_scan(u, gg, rg_conv_w, rg_conv_b3, wg_cat, bg_cat, rg_lam, j, batch, chunk)
            h, xn, lt = _mix_out(y, h, rg_w_out, j, rg_b_out3, g_ffn3, i, w_router_t, tm)
        else:
            ucs = _fourier_in(h, g_mix3, i, ft_w_in, j, ch_tab, batch, tm)
            f = _seq_fft(ucs, fft_tab_p, fft_tab_a, fft_tab_b, dft_scale, min(256, d))
            h, xn, lt = _mix_out(f.reshape(n, d), h, ft_w_out, j, ft_b_out3, g_ffn3, i,
                                 w_router_t, tm)

        psel, gc = _route(lt, batch, cap)
        st_d, nr_d = _tile_tables(psel, cap, tt_d, wp_d)
        st_c, nr_c = _tile_tables(psel, cap, tt_c, wp_c)
        psel_tm = jnp.swapaxes(psel, 1, 2).reshape(n, n_exp)
        xg = _dispatch(st_d, nr_d, xn, psel, batch, cap, tt_d, wp_d)
        yg = _ffn(xg, gc, w_gate, w_up, w_down, i, 4)
        h = _combine(st_c, nr_c, h, psel_tm, p3, yg.reshape(n_exp * batch * cap, d),
                     ple_w_proj, ple_w_gate, g_ple3, i, g_final, batch, cap, tt_c,
                     wp_c, final=(i == depth - 1))
    return h.reshape(batch, s_len, d)
```
